```python
import functools
import jax, jax.numpy as jnp
from jax import lax
import numpy as np

D_MODEL = 1024
BATCH = 8
SEQ = 2048
DEPTH = 1
DEC_BATCH = 128
DEC_SEQ = 8
PAST_LEN = 16384
PAGE_SIZE = 128

SB_HEAD_DIM = 64
SB_HEADS = (D_MODEL // 2) // SB_HEAD_DIM
SB_W = SB_HEADS * SB_HEAD_DIM
MLA_V_DIM = 64
MLA_HEADS = (D_MODEL // 2) // MLA_V_DIM
MLA_NOPE_DIM = 64
MLA_ROPE_DIM = 32
MLA_Q_LORA = 256
MLA_KV_LORA = 128
MLA_W = MLA_HEADS * MLA_V_DIM
MIX_W = SB_W + MLA_W
IN_W = 3 * SB_W + MLA_Q_LORA + MLA_KV_LORA + MLA_ROPE_DIM
D_FF = -(-8 * D_MODEL // (3 * 256)) * 256
PLE_DIM = 256
Q_BLOCK = 128
ROPE_THETA = 10000.0
NORM_EPS = 1e-6
SB_SCALE = SB_HEAD_DIM ** -0.5
MLA_SCALE = (MLA_NOPE_DIM + MLA_ROPE_DIM) ** -0.5

kernel_name = "hybrid_stickbreak_mla_decode_step"


def rmsnorm(x, g):
    xf = x.astype(jnp.float32)
    y = xf * lax.rsqrt(jnp.mean(xf * xf, axis=-1, keepdims=True) + NORM_EPS)
    return (y * g.astype(jnp.float32)).astype(x.dtype)


def rope(x, pos):
    r = x.shape[-1]
    half = r // 2
    inv = ROPE_THETA ** (-2.0 * jnp.arange(half, dtype=jnp.float32) / r)
    ang = pos.astype(jnp.float32)[:, None] * inv[None, :]
    cos = jnp.cos(ang)[None, :, None, :]
    sin = jnp.sin(ang)[None, :, None, :]
    xf = x.astype(jnp.float32)
    x1, x2 = xf[..., :half], xf[..., half:]
    return jnp.concatenate([x1 * cos - x2 * sin, x1 * sin + x2 * cos], axis=-1).astype(x.dtype)


def sb_attend(q, k, v, q_pos, k_pos):
    z = jnp.einsum('bqhd,bkhd->bhqk', q, k).astype(jnp.float32) * SB_SCALE
    mask = k_pos[None, :] < q_pos[:, None]
    log_beta = jax.nn.log_sigmoid(z)
    log_keep = jnp.where(mask, jax.nn.log_sigmoid(-z), 0.0)
    log_later = lax.cumsum(log_keep, axis=3, reverse=True) - log_keep
    w = jnp.where(mask, jnp.exp(log_beta + log_later), 0.0)
    return jnp.einsum('bhqk,bkhd->bqhd', w.astype(v.dtype), v)


def mla_attend(q_lat, q_pe, ckv, kpe, q_pos, k_pos):
    s = (jnp.einsum('bqhc,bkc->bhqk', q_lat, ckv)
         + jnp.einsum('bqhr,bkr->bhqk', q_pe, kpe)).astype(jnp.float32) * MLA_SCALE
    mask = k_pos[None, :] <= q_pos[:, None]
    p = jax.nn.softmax(jnp.where(mask, s, -jnp.inf), axis=-1)
    return jnp.einsum('bhqk,bkc->bqhc', p.astype(ckv.dtype), ckv)


def mixer_inputs(h, pos, w_in, norm_q_a, w_uq, norm_kv_a, w_uk):
    b, s, _ = h.shape
    proj = h @ w_in
    cuts = [SB_W, 2 * SB_W, 3 * SB_W, 3 * SB_W + MLA_Q_LORA, 3 * SB_W + MLA_Q_LORA + MLA_KV_LORA]
    q_sb, k_sb, v_sb, c_q, c_kv, k_pe = jnp.split(proj, cuts, axis=-1)
    q_sb = q_sb.reshape(b, s, SB_HEADS, SB_HEAD_DIM)
    k_sb = k_sb.reshape(b, s, SB_HEADS, SB_HEAD_DIM)
    v_sb = v_sb.reshape(b, s, SB_HEADS, SB_HEAD_DIM)
    q = (rmsnorm(c_q, norm_q_a) @ w_uq).reshape(b, s, MLA_HEADS, MLA_NOPE_DIM + MLA_ROPE_DIM)
    q_nope, q_pe = q[..., :MLA_NOPE_DIM], q[..., MLA_NOPE_DIM:]
    q_pe = rope(q_pe, pos)
    q_lat = jnp.einsum('bshn,chn->bshc', q_nope, w_uk)
    ckv = rmsnorm(c_kv, norm_kv_a)
    kpe = rope(k_pe[:, :, None, :], pos)[:, :, 0, :]
    return q_sb, k_sb, v_sb, q_lat, q_pe, ckv, kpe


def prompt_attend(q_sb, k_sb, v_sb, q_lat, q_pe, ckv, kpe):
    b, s = q_sb.shape[:2]
    nb = s // Q_BLOCK
    pos = jnp.arange(s, dtype=jnp.int32)

    def to_blocks(a):
        return jnp.moveaxis(a.reshape(b, nb, Q_BLOCK, *a.shape[2:]), 1, 0)

    def from_blocks(a):
        return jnp.moveaxis(a, 0, 1).reshape(b, s, *a.shape[3:])

    def block(args):
        qs, ql, qp, qpos = args
        return (sb_attend(qs, k_sb, v_sb, qpos, pos),
                mla_attend(ql, qp, ckv, kpe, qpos, pos))

    sb_o, lat_o = lax.map(block, (to_blocks(q_sb), to_blocks(q_lat), to_blocks(q_pe),
                                  pos.reshape(nb, Q_BLOCK)))
    return from_blocks(sb_o), from_blocks(lat_o)


def sample_attend(q_sb, k_sb, v_sb, q_lat, q_pe, ckv, kpe, c_k, c_v, c_ckv, c_kpe, page_table):
    t = q_sb.shape[1]
    past = page_table.shape[1] * c_k.shape[1]
    q_pos = past + jnp.arange(t, dtype=jnp.int32)
    k_pos = jnp.arange(past + t, dtype=jnp.int32)

    def gather(cache, pages):
        return cache[pages].reshape(past, *cache.shape[2:])

    def one_seq(args):
        qs, ks, vs, ql, qp, c, kp, pages = args
        kk = jnp.concatenate([gather(c_k, pages), ks], axis=0)[None]
        vv = jnp.concatenate([gather(c_v, pages), vs], axis=0)[None]
        cc = jnp.concatenate([gather(c_ckv, pages), c], axis=0)[None]
        pp = jnp.concatenate([gather(c_kpe, pages), kp], axis=0)[None]
        return (sb_attend(qs[None], kk, vv, q_pos, k_pos)[0],
                mla_attend(ql[None], qp[None], cc, pp, q_pos, k_pos)[0])

    return lax.map(one_seq, (q_sb, k_sb, v_sb, q_lat, q_pe, ckv, kpe, page_table))


def trunk_layer(x, p, pos, attend, lw):
    (norm_attn, w_in, norm_q_a, w_uq, norm_kv_a, w_uk, w_uv, norm_sb_out, norm_mla_out, w_o,
     norm_ffn, w_gate, w_up, w_down, norm_ple, w_ple_gate, w_ple_proj) = lw
    b, s, _ = x.shape
    h = rmsnorm(x, norm_attn)
    q_sb, k_sb, v_sb, q_lat, q_pe, ckv, kpe = mixer_inputs(h, pos, w_in, norm_q_a, w_uq, norm_kv_a, w_uk)
    sb_o, lat_o = attend(q_sb, k_sb, v_sb, q_lat, q_pe, ckv, kpe)
    mla_o = jnp.einsum('bshc,chv->bshv', lat_o, w_uv).reshape(b, s, MLA_W)
    merged = jnp.concatenate([rmsnorm(sb_o.reshape(b, s, SB_W), norm_sb_out),
                              rmsnorm(mla_o, norm_mla_out)], axis=-1)
    x = x + merged @ w_o
    h = rmsnorm(x, norm_ffn)
    x = x + (jax.nn.silu(h @ w_gate) * (h @ w_up)) @ w_down
    gate = jax.nn.sigmoid(rmsnorm(x, norm_ple) @ w_ple_gate)
    x = x + gate * (p @ w_ple_proj)
    return x, (k_sb, v_sb, ckv, kpe)


def setup_inputs(seed: int = 0) -> dict:
    key = jax.random.key(seed)
    ks = iter(jax.random.split(key, 40))
    f32 = jnp.float32
    n_pages = PAST_LEN // PAGE_SIZE
    n_used = DEC_BATCH * n_pages
    n_phys = n_used + max(1, n_used // 4)

    def nrm(shape, scale=1.0):
        return jax.random.normal(next(ks), shape, f32) * scale

    def gain(shape):
        return 1.0 + 0.1 * jax.random.normal(next(ks), shape, f32)

    d = {}
    d['x_prompt'] = nrm((BATCH, SEQ, D_MODEL))
    d['x_sample'] = nrm((DEC_BATCH, DEC_SEQ, D_MODEL))
    d['cache_sb_k'] = nrm((DEPTH, n_phys, PAGE_SIZE, SB_HEADS, SB_HEAD_DIM))
    d['cache_sb_v'] = nrm((DEPTH, n_phys, PAGE_SIZE, SB_HEADS, SB_HEAD_DIM))
    d['cache_mla_ckv'] = nrm((DEPTH, n_phys, PAGE_SIZE, MLA_KV_LORA))
    d['cache_mla_kpe'] = nrm((DEPTH, n_phys, PAGE_SIZE, MLA_ROPE_DIM))
    perm = jax.random.permutation(next(ks), n_phys)[:n_used]
    d['page_table'] = perm.reshape(DEC_BATCH, n_pages).astype(jnp.int32)
    d['p_prompt'] = nrm((DEPTH, BATCH, SEQ, PLE_DIM))
    d['p_sample'] = nrm((DEPTH, DEC_BATCH, DEC_SEQ, PLE_DIM))
    d['norm_attn'] = gain((DEPTH, D_MODEL))
    d['w_in'] = nrm((DEPTH, D_MODEL, IN_W), D_MODEL ** -0.5)
    d['norm_q_a'] = gain((DEPTH, MLA_Q_LORA))
    d['w_uq'] = nrm((DEPTH, MLA_Q_LORA, MLA_HEADS * (MLA_NOPE_DIM + MLA_ROPE_DIM)), MLA_Q_LORA ** -0.5)
    d['norm_kv_a'] = gain((DEPTH, MLA_KV_LORA))
    d['w_uk'] = nrm((DEPTH, MLA_KV_LORA, MLA_HEADS, MLA_NOPE_DIM), MLA_KV_LORA ** -0.5)
    d['w_uv'] = nrm((DEPTH, MLA_KV_LORA, MLA_HEADS, MLA_V_DIM), MLA_KV_LORA ** -0.5)
    d['norm_sb_out'] = gain((DEPTH, SB_W))
    d['norm_mla_out'] = gain((DEPTH, MLA_W))
    d['w_o'] = nrm((DEPTH, MIX_W, D_MODEL), MIX_W ** -0.5)
    d['norm_ffn'] = gain((DEPTH, D_MODEL))
    d['w_gate'] = nrm((DEPTH, D_MODEL, D_FF), D_MODEL ** -0.5)
    d['w_up'] = nrm((DEPTH, D_MODEL, D_FF), D_MODEL ** -0.5)
    d['w_down'] = nrm((DEPTH, D_FF, D_MODEL), D_FF ** -0.5)
    d['norm_ple'] = gain((DEPTH, D_MODEL))
    d['w_ple_gate'] = nrm((DEPTH, D_MODEL, D_MODEL), D_MODEL ** -0.5)
    d['w_ple_proj'] = nrm((DEPTH, PLE_DIM, D_MODEL), PLE_DIM ** -0.5)
    d['norm_final'] = gain((D_MODEL,))
    return d


def reference(x_prompt, x_sample, cache_sb_k, cache_sb_v, cache_mla_ckv, cache_mla_kpe, page_table,
              p_prompt, p_sample, norm_attn, w_in, norm_q_a, w_uq, norm_kv_a, w_uk, w_uv,
              norm_sb_out, norm_mla_out, w_o, norm_ffn, w_gate, w_up, w_down, norm_ple,
              w_ple_gate, w_ple_proj, norm_final):
    past = page_table.shape[1] * cache_sb_k.shape[2]
    pos_prompt = jnp.arange(x_prompt.shape[1], dtype=jnp.int32)
    pos_sample = past + jnp.arange(x_sample.shape[1], dtype=jnp.int32)
    xp, xs = x_prompt, x_sample
    st_prompt, st_sample = [], []
    for i in range(DEPTH):
        lw = (norm_attn[i], w_in[i], norm_q_a[i], w_uq[i], norm_kv_a[i], w_uk[i], w_uv[i],
              norm_sb_out[i], norm_mla_out[i], w_o[i], norm_ffn[i], w_gate[i], w_up[i], w_down[i],
              norm_ple[i], w_ple_gate[i], w_ple_proj[i])
        xp, stp = trunk_layer(xp, p_prompt[i], pos_prompt, prompt_attend, lw)
        attend_sample = functools.partial(sample_attend, c_k=cache_sb_k[i], c_v=cache_sb_v[i],
                                          c_ckv=cache_mla_ckv[i], c_kpe=cache_mla_kpe[i],
                                          page_table=page_table)
        xs, sts = trunk_layer(xs, p_sample[i], pos_sample, attend_sample, lw)
        st_prompt.append(stp)
        st_sample.append(sts)
    y_prompt = rmsnorm(xp, norm_final)
    y_sample = rmsnorm(xs, norm_final)
    new_sb_k_prompt = jnp.stack([s[0] for s in st_prompt])
    new_sb_v_prompt = jnp.stack([s[1] for s in st_prompt])
    new_ckv_prompt = jnp.stack([s[2] for s in st_prompt])
    new_kpe_prompt = jnp.stack([s[3] for s in st_prompt])
    new_sb_k_sample = jnp.stack([s[0] for s in st_sample])
    new_sb_v_sample = jnp.stack([s[1] for s in st_sample])
    new_ckv_sample = jnp.stack([s[2] for s in st_sample])
    new_kpe_sample = jnp.stack([s[3] for s in st_sample])
    return (y_prompt, y_sample, new_sb_k_prompt, new_sb_v_prompt, new_ckv_prompt, new_kpe_prompt,
            new_sb_k_sample, new_sb_v_sample, new_ckv_sample, new_kpe_sample)
```

```python
import functools

import jax
import jax.numpy as jnp
from jax import lax
from jax.experimental import pallas as pl
from jax.experimental.pallas import tpu as pltpu

F32 = jnp.float32
BF16 = jnp.bfloat16

LANES = 128
HEADS = 8
SB_D = 64
SB_W = HEADS * SB_D
NOPE = 64
ROPE = 32
Q_LORA = 256
KV_LORA = 128
V_DIM = 64
MLA_W = HEADS * V_DIM
FF_CHUNK = 256
ROPE_THETA = 10000.0
NORM_EPS = 1e-6
SB_SCALE = SB_D ** -0.5
MLA_SCALE = (NOPE + ROPE) ** -0.5
KEY_BLOCK = 128
SB_DEAD_LOG = 104.0
VMEM_LIMIT = 56 * 1024 * 1024


def _rms(x, g):
    return x * lax.rsqrt(jnp.mean(x * x, axis=-1, keepdims=True) + NORM_EPS) * g


def _dot(a, b):
    return jnp.dot(a, b, preferred_element_type=F32)


def _dot_nt(a, b):
    return lax.dot_general(a, b, (((1,), (1,)), ((), ())), preferred_element_type=F32)


def _const_spec(shape):
    nd = len(shape)
    return pl.BlockSpec(shape, lambda *_: (0,) * nd)


def _pre_kernel(x_ref, cos_ref, sin_ref, ga_ref, gq_ref, gkv_ref, w1_ref, w2_ref, wuk_ref,
                qx_ref, k_ref, v_ref, kb_ref, vb_ref, ckv_ref, kpe_ref, kv_ref, qm_ref):
    x = x_ref[...]
    tm = x.shape[0]
    h = _rms(x, ga_ref[...]).astype(BF16)
    proj = _dot(h, w1_ref[...])
    lane = lax.broadcasted_iota(jnp.int32, (tm, LANES), 1)
    low = lane < SB_D
    for hp in range(HEADS // 2):
        blk = proj[:, hp * LANES:(hp + 1) * LANES] * SB_SCALE
        qx_ref[:, (2 * hp) * LANES:(2 * hp + 1) * LANES] = jnp.where(low, blk, 0.0).astype(BF16)
        qx_ref[:, (2 * hp + 1) * LANES:(2 * hp + 2) * LANES] = jnp.where(low, 0.0, blk).astype(BF16)
    k = proj[:, SB_W:2 * SB_W]
    v = proj[:, 2 * SB_W:3 * SB_W]
    k_ref[...] = k
    v_ref[...] = v
    kb_ref[...] = k.astype(BF16)
    vb_ref[...] = v.astype(BF16)
    o = 3 * SB_W
    cqn = _rms(proj[:, o:o + Q_LORA], gq_ref[...]).astype(BF16)
    o += Q_LORA
    ckv = _rms(proj[:, o:o + KV_LORA], gkv_ref[...])
    o += KV_LORA
    cos = cos_ref[...]
    sin = sin_ref[...]
    kpe = proj[:, o:o + LANES] * cos + proj[:, o + LANES:o + 2 * LANES] * sin
    ckv_ref[...] = ckv
    kpe_ref[...] = kpe[:, :ROPE]
    kv_ref[:, :KV_LORA] = ckv.astype(BF16)
    kv_ref[:, KV_LORA:] = kpe.astype(BF16)
    q2 = _dot(cqn, w2_ref[...])
    lat = _dot(q2[:, :HEADS * NOPE].astype(BF16), wuk_ref[...])
    o1 = HEADS * NOPE
    o2 = o1 + HEADS * LANES
    for hh in range(HEADS):
        pe = q2[:, o1 + hh * LANES:o1 + (hh + 1) * LANES] * cos + q2[:, o2 + hh * LANES:o2 + (hh + 1) * LANES] * sin
        qm_ref[:, hh * 2 * LANES:hh * 2 * LANES + LANES] = (lat[:, hh * LANES:(hh + 1) * LANES] * MLA_SCALE).astype(BF16)
        qm_ref[:, hh * 2 * LANES + LANES:(hh + 1) * 2 * LANES] = (pe * MLA_SCALE).astype(BF16)


def _pre(x, cos_t, sin_t, n_tab_blocks, ga, gq, gkv, w1, w2, wuk, tm):
    t, d = x.shape
    grid = (t // tm,)
    row = lambda w: pl.BlockSpec((tm, w), lambda i: (i, 0))
    tab = pl.BlockSpec((tm, LANES), lambda i: (i % n_tab_blocks, 0))
    outs = [
        jax.ShapeDtypeStruct((t, HEADS * LANES), BF16),
        jax.ShapeDtypeStruct((t, SB_W), F32),
        jax.ShapeDtypeStruct((t, SB_W), F32),
        jax.ShapeDtypeStruct((t, SB_W), BF16),
        jax.ShapeDtypeStruct((t, SB_W), BF16),
        jax.ShapeDtypeStruct((t, KV_LORA), F32),
        jax.ShapeDtypeStruct((t, ROPE), F32),
        jax.ShapeDtypeStruct((t, 2 * LANES), BF16),
        jax.ShapeDtypeStruct((t, HEADS * 2 * LANES), BF16),
    ]
    return pl.pallas_call(
        _pre_kernel,
        grid=grid,
        in_specs=[row(d), tab, tab, _const_spec(ga.shape), _const_spec(gq.shape), _const_spec(gkv.shape),
                  _const_spec(w1.shape), _const_spec(w2.shape), _const_spec(wuk.shape)],
        out_specs=[row(o.shape[1]) for o in outs],
        out_shape=outs,
        compiler_params=pltpu.CompilerParams(dimension_semantics=("arbitrary",), vmem_limit_bytes=VMEM_LIMIT),
        name="pre",
    )(x, cos_t, sin_t, ga, gq, gkv, w1, w2, wuk)


def _sb_block(z, mask, u2, carry):
    l1p = jnp.log1p(jnp.exp(-jnp.abs(z)))
    log_beta = jnp.minimum(z, 0.0) - l1p
    log_keep = jnp.where(mask, jnp.minimum(-z, 0.0) - l1p, 0.0)
    hi = log_keep.astype(BF16)
    lo = (log_keep - hi.astype(F32)).astype(BF16)
    cs = _dot(hi, u2) + _dot(lo, u2)
    w = jnp.where(mask, jnp.exp(log_beta + cs[:, :LANES] + carry), 0.0)
    return w, carry + cs[:, LANES:]


def _sb_prompt_kernel(q_ref, k_ref, v_ref, u2_ref, o_ref, acc_ref, car_ref, *, tq):
    qi = pl.program_id(1)
    lane = lax.broadcasted_iota(jnp.int32, (tq, LANES), 1)
    row = lax.broadcasted_iota(jnp.int32, (tq, LANES), 0)
    acc_ref[...] = jnp.zeros_like(acc_ref)
    car_ref[...] = jnp.zeros_like(car_ref)
    u2 = u2_ref[...]

    def body(state):
        j, _ = state
        mask = lane < row + (qi - j) * KEY_BLOCK
        start = pl.multiple_of(j * KEY_BLOCK, KEY_BLOCK)
        cmax = None
        for hp in range(HEADS // 2):
            kblk = k_ref[0, pl.ds(start, KEY_BLOCK), hp * LANES:(hp + 1) * LANES]
            vblk = v_ref[0, pl.ds(start, KEY_BLOCK), hp * LANES:(hp + 1) * LANES]
            for e in range(2):
                hh = 2 * hp + e
                z = _dot_nt(q_ref[0, :, hh * LANES:(hh + 1) * LANES], kblk)
                w, car = _sb_block(z, mask, u2, car_ref[hh])
                acc_ref[hh] += _dot(w.astype(BF16), vblk)
                car_ref[hh] = car
                cmax = car if cmax is None else jnp.maximum(cmax, car)
        return j - 1, jnp.max(cmax) < -SB_DEAD_LOG

    lax.while_loop(lambda s: (s[0] >= 0) & jnp.logical_not(s[1]), body, (qi, False))
    for hp in range(HEADS // 2):
        o_ref[0, :, hp * LANES:(hp + 1) * LANES] = jnp.where(lane < SB_D, acc_ref[2 * hp], acc_ref[2 * hp + 1])


def _sb_prompt(qx, kb, vb, u2, tq):
    b, s, _ = qx.shape
    return pl.pallas_call(
        functools.partial(_sb_prompt_kernel, tq=tq),
        grid=(b, s // tq),
        in_specs=[pl.BlockSpec((1, tq, HEADS * LANES), lambda bi, qi: (bi, qi, 0)),
                  pl.BlockSpec((1, s, SB_W), lambda bi, qi: (bi, 0, 0)),
                  pl.BlockSpec((1, s, SB_W), lambda bi, qi: (bi, 0, 0)),
                  _const_spec(u2.shape)],
        out_specs=pl.BlockSpec((1, tq, SB_W), lambda bi, qi: (bi, qi, 0)),
        out_shape=jax.ShapeDtypeStruct((b, s, SB_W), F32),
        scratch_shapes=[pltpu.VMEM((HEADS, tq, LANES), F32), pltpu.VMEM((HEADS, tq, LANES), F32)],
        compiler_params=pltpu.CompilerParams(dimension_semantics=("arbitrary", "arbitrary"),
                                             vmem_limit_bytes=VMEM_LIMIT),
        name="sb_prompt",
    )(qx, kb, vb, u2)


def _softmax_step(s, kv_lat, m_ref, l_ref, acc_ref):
    m_prev = m_ref[...]
    m_new = jnp.maximum(m_prev, jnp.max(s, axis=-1, keepdims=True))
    p = jnp.exp(s - m_new)
    alpha = jnp.exp(m_prev - m_new)
    l_ref[...] = alpha * l_ref[...] + jnp.sum(p, axis=-1, keepdims=True)
    acc_ref[...] = alpha * acc_ref[...] + _dot(p.astype(BF16), kv_lat)
    m_ref[...] = m_new


def _mla_prompt_kernel(qm_ref, kv_ref, o_ref, qs_ref, m_ref, l_ref, acc_ref, *, tq):
    qi = pl.program_id(1)
    rows = HEADS * tq
    for hh in range(HEADS):
        qs_ref[hh * tq:(hh + 1) * tq, :] = qm_ref[0, :, hh * 2 * LANES:(hh + 1) * 2 * LANES]
    m_ref[...] = jnp.full_like(m_ref, -jnp.inf)
    l_ref[...] = jnp.zeros_like(l_ref)
    acc_ref[...] = jnp.zeros_like(acc_ref)
    col = lax.broadcasted_iota(jnp.int32, (rows, LANES), 1)
    row = lax.broadcasted_iota(jnp.int32, (rows, LANES), 0) & (tq - 1)

    def body(j, carry):
        start = pl.multiple_of(j * KEY_BLOCK, KEY_BLOCK)
        kv = kv_ref[0, pl.ds(start, KEY_BLOCK), :]
        s = _dot_nt(qs_ref[...], kv)
        s = jnp.where(col <= row + (qi - j) * KEY_BLOCK, s, -jnp.inf)
        _softmax_step(s, kv[:, :KV_LORA], m_ref, l_ref, acc_ref)
        return carry

    lax.fori_loop(0, qi + 1, body, 0)
    out = acc_ref[...] / l_ref[...]
    for hh in range(HEADS):
        o_ref[0, :, hh * LANES:(hh + 1) * LANES] = out[hh * tq:(hh + 1) * tq]


def _mla_prompt(qm, kv, tq):
    b, s, _ = qm.shape
    rows = HEADS * tq
    return pl.pallas_call(
        functools.partial(_mla_prompt_kernel, tq=tq),
        grid=(b, s // tq),
        in_specs=[pl.BlockSpec((1, tq, HEADS * 2 * LANES), lambda bi, qi: (bi, qi, 0)),
                  pl.BlockSpec((1, s, 2 * LANES), lambda bi, qi: (bi, 0, 0))],
        out_specs=pl.BlockSpec((1, tq, HEADS * LANES), lambda bi, qi: (bi, qi, 0)),
        out_shape=jax.ShapeDtypeStruct((b, s, HEADS * LANES), F32),
        scratch_shapes=[pltpu.VMEM((rows, 2 * LANES), BF16), pltpu.VMEM((rows, LANES), F32),
                        pltpu.VMEM((rows, LANES), F32), pltpu.VMEM((rows, LANES), F32)],
        compiler_params=pltpu.CompilerParams(dimension_semantics=("arbitrary", "arbitrary"),
                                             vmem_limit_bytes=VMEM_LIMIT),
        name="mla_prompt",
    )(qm, kv)


def _sb_sample_kernel(pt_ref, q_ref, kn_ref, vn_ref, u2_ref, ck_ref, cv_ref, o_ref,
                      kbuf, vbuf, sem, acc_ref, car_ref, *, n_pages, t_new):
    n = pl.program_id(0)
    rows = HEADS * t_new
    u2 = u2_ref[...]
    lane = lax.broadcasted_iota(jnp.int32, (rows, LANES), 1)
    tq = lax.broadcasted_iota(jnp.int32, (rows, LANES), 0) & (t_new - 1)

    def page_copies(i, slot):
        page = pt_ref[n, n_pages - 1 - i]
        cps = []
        for hh in range(HEADS):
            cps.append(pltpu.make_async_copy(ck_ref.at[page, :, hh, :], kbuf.at[slot, hh], sem.at[slot, 0]))
            cps.append(pltpu.make_async_copy(cv_ref.at[page, :, hh, :], vbuf.at[slot, hh], sem.at[slot, 1]))
        return cps

    def start_page(i, slot):
        for cp in page_copies(i, slot):
            cp.start()

    def wait_page(i, slot):
        for cp in page_copies(i, slot):
            cp.wait()

    start_page(0, 0)

    def attend(k_of, v_of, mask, car):
        z = jnp.concatenate([_dot_nt(q_ref[0, hh], k_of(hh)) for hh in range(HEADS)], axis=0)
        w, car = _sb_block(z, mask, u2, car)
        for hh in range(HEADS):
            acc_ref[hh] += _dot(w[hh * t_new:(hh + 1) * t_new].astype(BF16), v_of(hh))
        return car

    acc_ref[...] = jnp.zeros_like(acc_ref)
    car0 = attend(lambda hh: kn_ref[0, hh], lambda hh: vn_ref[0, hh], lane < tq, jnp.zeros((rows, LANES), F32))
    car_ref[...] = car0
    all_keys = lane >= 0

    def body(state):
        i, _ = state
        slot = i % 2
        wait_page(i, slot)

        @pl.when(i + 1 < n_pages)
        def _():
            start_page(i + 1, 1 - slot)

        car = attend(lambda hh: kbuf[slot, hh].astype(BF16), lambda hh: vbuf[slot, hh].astype(BF16),
                     all_keys, car_ref[...])
        car_ref[...] = car
        return i + 1, jnp.max(car) < -SB_DEAD_LOG

    i_end, _ = lax.while_loop(lambda s: (s[0] < n_pages) & jnp.logical_not(s[1]), body,
                              (0, jnp.max(car0) < -SB_DEAD_LOG))

    @pl.when(i_end < n_pages)
    def _():
        wait_page(i_end, i_end % 2)

    o_ref[0] = acc_ref[...]


def _sb_sample(page_table, q, kn, vn, u2, cache_k, cache_v):
    n, _, t_new, _ = q.shape
    n_pages = page_table.shape[1]
    page = cache_k.shape[1]
    assert page == KEY_BLOCK
    grid_spec = pltpu.PrefetchScalarGridSpec(
        num_scalar_prefetch=1,
        grid=(n,),
        in_specs=[pl.BlockSpec((1, HEADS, t_new, SB_D), lambda i, pt: (i, 0, 0, 0)),
                  pl.BlockSpec((1, HEADS, KEY_BLOCK, SB_D), lambda i, pt: (i, 0, 0, 0)),
                  pl.BlockSpec((1, HEADS, KEY_BLOCK, SB_D), lambda i, pt: (i, 0, 0, 0)),
                  pl.BlockSpec(u2.shape, lambda i, pt: (0, 0)),
                  pl.BlockSpec(memory_space=pl.ANY),
                  pl.BlockSpec(memory_space=pl.ANY)],
        out_specs=pl.BlockSpec((1, HEADS, t_new, SB_D), lambda i, pt: (i, 0, 0, 0)),
        scratch_shapes=[pltpu.VMEM((2, HEADS, KEY_BLOCK, SB_D), F32),
                        pltpu.VMEM((2, HEADS, KEY_BLOCK, SB_D), F32),
                        pltpu.SemaphoreType.DMA((2, 2)),
                        pltpu.VMEM((HEADS, t_new, SB_D), F32),
                        pltpu.VMEM((HEADS * t_new, LANES), F32)],
    )
    return pl.pallas_call(
        functools.partial(_sb_sample_kernel, n_pages=n_pages, t_new=t_new),
        grid_spec=grid_spec,
        out_shape=jax.ShapeDtypeStruct((n, HEADS, t_new, SB_D), F32),
        compiler_params=pltpu.CompilerParams(dimension_semantics=("arbitrary",), vmem_limit_bytes=VMEM_LIMIT),
        name="sb_sample",
    )(page_table, q, kn, vn, u2, cache_k, cache_v)


def _mla_sample_kernel(pt_ref, q_ref, kvn_ref, *refs, group, t_new):
    ckv_refs = refs[:group]
    kpe_refs = refs[group:2 * group]
    o_ref, m_ref, l_ref, acc_ref = refs[2 * group:]
    j = pl.program_id(1)
    rows = HEADS * t_new
    q = q_ref[0]

    @pl.when(j == 0)
    def _():
        m_ref[...] = jnp.full_like(m_ref, -jnp.inf)
        l_ref[...] = jnp.zeros_like(l_ref)
        acc_ref[...] = jnp.zeros_like(acc_ref)
        col = lax.broadcasted_iota(jnp.int32, (rows, LANES), 1)
        tq = lax.broadcasted_iota(jnp.int32, (rows, LANES), 0) & (t_new - 1)
        kvn = kvn_ref[0]
        s = jnp.where(col <= tq, _dot_nt(q, kvn), -jnp.inf)
        _softmax_step(s, kvn[:, :KV_LORA], m_ref, l_ref, acc_ref)

    q_lat = q[:, :KV_LORA]
    q_pe = q[:, KV_LORA:KV_LORA + ROPE]
    for g in range(group):
        ckv = ckv_refs[g][...].astype(BF16)
        kpe = kpe_refs[g][...].astype(BF16)
        s = _dot_nt(q_lat, ckv) + _dot_nt(q_pe, kpe)
        _softmax_step(s, ckv, m_ref, l_ref, acc_ref)

    @pl.when(j == pl.num_programs(1) - 1)
    def _():
        o_ref[0] = acc_ref[...] / l_ref[...]


def _mla_sample(page_table, q, kvn, cache_ckv, cache_kpe, group):
    n, rows, _ = q.shape
    t_new = rows // HEADS
    n_pages = page_table.shape[1]
    assert n_pages % group == 0 and cache_ckv.shape[1] == KEY_BLOCK

    def page_spec(width, g):
        return pl.BlockSpec((None, KEY_BLOCK, width), lambda i, j, pt: (pt[i, j * group + g], 0, 0))

    grid_spec = pltpu.PrefetchScalarGridSpec(
        num_scalar_prefetch=1,
        grid=(n, n_pages // group),
        in_specs=[pl.BlockSpec((1, rows, 2 * LANES), lambda i, j, pt: (i, 0, 0)),
                  pl.BlockSpec((1, KEY_BLOCK, 2 * LANES), lambda i, j, pt: (i, 0, 0))]
                 + [page_spec(KV_LORA, g) for g in range(group)]
                 + [page_spec(ROPE, g) for g in range(group)],
        out_specs=pl.BlockSpec((1, rows, KV_LORA), lambda i, j, pt: (i, 0, 0)),
        scratch_shapes=[pltpu.VMEM((rows, LANES), F32), pltpu.VMEM((rows, LANES), F32),
                        pltpu.VMEM((rows, KV_LORA), F32)],
    )
    return pl.pallas_call(
        functools.partial(_mla_sample_kernel, group=group, t_new=t_new),
        grid_spec=grid_spec,
        out_shape=jax.ShapeDtypeStruct((n, rows, KV_LORA), F32),
        compiler_params=pltpu.CompilerParams(dimension_semantics=("arbitrary", "arbitrary"),
                                             vmem_limit_bytes=VMEM_LIMIT),
        name="mla_sample",
    )(page_table, q, kvn, *([cache_ckv] * group), *([cache_kpe] * group))


def _post_kernel(sb_ref, lat_ref, x_ref, p_ref, wuv_ref, gsb_ref, gmla_ref, wo_ref, gffn_ref, wg_ref, wu_ref,
                 wd_ref, gple_ref, wpg_ref, wpp_ref, gfin_ref, y_ref, mrg_ref, ff_ref, *, n_chunks):
    mla_o = _dot(lat_ref[...].astype(BF16), wuv_ref[...])
    mrg_ref[:, :SB_W] = _rms(sb_ref[...], gsb_ref[...]).astype(BF16)
    mrg_ref[:, SB_W:] = _rms(mla_o, gmla_ref[...]).astype(BF16)
    x1 = x_ref[...] + _dot(mrg_ref[...], wo_ref[...])
    h = _rms(x1, gffn_ref[...]).astype(BF16)
    ff_ref[...] = jnp.zeros_like(ff_ref)

    def chunk(c, carry):
        g = _dot(h, wg_ref[c])
        u = _dot(h, wu_ref[c])
        a = (g * jax.nn.sigmoid(g) * u).astype(BF16)
        ff_ref[...] += _dot(a, wd_ref[c])
        return carry

    lax.fori_loop(0, n_chunks, chunk, 0)
    x2 = x1 + ff_ref[...]
    gate = jax.nn.sigmoid(_dot(_rms(x2, gple_ref[...]).astype(BF16), wpg_ref[...]))
    x3 = x2 + gate * _dot(p_ref[...].astype(BF16), wpp_ref[...])
    y_ref[...] = _rms(x3, gfin_ref[...])


def _post(sb_o, lat_o, x, p, wuv, gsb, gmla, wo, gffn, wg, wu, wd, gple, wpg, wpp, gfin, tm):
    t, d = x.shape
    row = lambda w: pl.BlockSpec((tm, w), lambda i: (i, 0))

    def resident(a):
        nd = a.ndim
        return pl.BlockSpec(a.shape, lambda i: (0,) * nd, pipeline_mode=pl.Buffered(1))

    weights = (wuv, gsb, gmla, wo, gffn, wg, wu, wd, gple, wpg, wpp, gfin)
    return pl.pallas_call(
        functools.partial(_post_kernel, n_chunks=wg.shape[0]),
        grid=(t // tm,),
        in_specs=[row(SB_W), row(HEADS * LANES), row(d), row(p.shape[1])] + [resident(a) for a in weights],
        out_specs=row(d),
        out_shape=jax.ShapeDtypeStruct((t, d), F32),
        scratch_shapes=[pltpu.VMEM((tm, SB_W + MLA_W), BF16), pltpu.VMEM((tm, d), F32)],
        compiler_params=pltpu.CompilerParams(dimension_semantics=("arbitrary",), vmem_limit_bytes=VMEM_LIMIT),
        name="post",
    )(sb_o, lat_o, x, p, *weights)


def _rope_tables(pos):
    half = ROPE // 2
    inv = ROPE_THETA ** (-2.0 * jnp.arange(half, dtype=F32) / ROPE)
    ang = pos.astype(F32)[:, None] * inv[None, :]
    cos, sin = jnp.cos(ang), jnp.sin(ang)
    pad = jnp.zeros((pos.shape[0], LANES - ROPE), F32)
    return (jnp.concatenate([cos, cos, pad], axis=1), jnp.concatenate([-sin, sin, pad], axis=1))


def _swap_halves(w):
    half = ROPE // 2
    return jnp.concatenate([w[..., half:], w[..., :half]], axis=-1)


def _place(w):
    return jnp.pad(w, [(0, 0)] * (w.ndim - 1) + [(0, LANES - ROPE)])


def _block_diag(w):
    hh, a, b = w.shape
    eye = jnp.eye(hh, dtype=w.dtype)
    return (eye[:, None, :, None] * w[:, :, None, :]).reshape(hh * a, hh * b)


def _token_tile(t):
    for tm in (256, 128, 64, 32, 16, 8):
        if t % tm == 0:
            return tm
    raise ValueError(f"token count {t} is not a multiple of 8")


def kernel(x_prompt, x_sample, cache_sb_k, cache_sb_v, cache_mla_ckv, cache_mla_kpe, page_table, p_prompt, p_sample, norm_attn, w_in, norm_q_a, w_uq, norm_kv_a, w_uk, w_uv, norm_sb_out, norm_mla_out, w_o, norm_ffn, w_gate, w_up, w_down, norm_ple, w_ple_gate, w_ple_proj, norm_final):
    depth = w_in.shape[0]
    assert depth == 1, "single trunk layer"
    b, s, d = x_prompt.shape
    n, t_new, _ = x_sample.shape
    n_pages = page_table.shape[1]
    past = n_pages * cache_sb_k.shape[2]
    assert s % KEY_BLOCK == 0 and t_new == 8 and cache_sb_k.shape[2] == KEY_BLOCK

    row2 = lambda g: g.reshape(1, -1).astype(F32)
    wi = w_in[0]
    o_kpe = 3 * SB_W + Q_LORA + KV_LORA
    w_kpe = wi[:, o_kpe:o_kpe + ROPE]
    w1 = jnp.concatenate([wi[:, :o_kpe], _place(w_kpe), _place(_swap_halves(w_kpe))], axis=1).astype(BF16)
    wq = w_uq[0].reshape(Q_LORA, HEADS, NOPE + ROPE)
    wq_pe = wq[:, :, NOPE:]
    w2 = jnp.concatenate([wq[:, :, :NOPE].reshape(Q_LORA, HEADS * NOPE),
                          _place(wq_pe).reshape(Q_LORA, HEADS * LANES),
                          _place(_swap_halves(wq_pe)).reshape(Q_LORA, HEADS * LANES)], axis=1).astype(BF16)
    wuk = _block_diag(jnp.transpose(w_uk[0], (1, 2, 0))).astype(BF16)
    wuv = _block_diag(jnp.transpose(w_uv[0], (1, 0, 2))).astype(BF16)
    n_chunks = w_gate.shape[2] // FF_CHUNK
    wg = jnp.transpose(w_gate[0].reshape(d, n_chunks, FF_CHUNK), (1, 0, 2)).astype(BF16)
    wu = jnp.transpose(w_up[0].reshape(d, n_chunks, FF_CHUNK), (1, 0, 2)).astype(BF16)
    wd = w_down[0].reshape(n_chunks, FF_CHUNK, d).astype(BF16)
    post_w = (wuv, row2(norm_sb_out[0]), row2(norm_mla_out[0]), w_o[0].astype(BF16), row2(norm_ffn[0]), wg, wu, wd,
              row2(norm_ple[0]), w_ple_gate[0].astype(BF16), w_ple_proj[0].astype(BF16), row2(norm_final))
    pre_w = (row2(norm_attn[0]), row2(norm_q_a[0]), row2(norm_kv_a[0]), w1, w2, wuk)
    ki = lax.broadcasted_iota(jnp.int32, (KEY_BLOCK, 2 * LANES), 0)
    si = lax.broadcasted_iota(jnp.int32, (KEY_BLOCK, 2 * LANES), 1)
    u2 = ((ki > si) | (si >= LANES)).astype(BF16)

    tm_p = _token_tile(s)
    cos_p, sin_p = _rope_tables(jnp.arange(s, dtype=jnp.int32))
    xp = x_prompt.reshape(b * s, d)
    qx, k_p, v_p, kb, vb, ckv_p, kpe_p, kv_p, qm = _pre(xp, cos_p, sin_p, s // tm_p, *pre_w, tm=tm_p)
    sb_o = _sb_prompt(qx.reshape(b, s, -1), kb.reshape(b, s, -1), vb.reshape(b, s, -1), u2, tq=KEY_BLOCK)
    lat_o = _mla_prompt(qm.reshape(b, s, -1), kv_p.reshape(b, s, -1), tq=KEY_BLOCK)
    y_p = _post(sb_o.reshape(b * s, -1), lat_o.reshape(b * s, -1), xp, p_prompt[0].reshape(b * s, -1), *post_w,
                tm=tm_p)

    ts = n * t_new
    tm_s = _token_tile(ts)
    cos_s, sin_s = _rope_tables(past + (jnp.arange(tm_s, dtype=jnp.int32) % t_new))
    xs = x_sample.reshape(ts, d)
    qx_s, k_s, v_s, kb_s, vb_s, ckv_s, kpe_s, kv_s, qm_s = _pre(xs, cos_s, sin_s, 1, *pre_w, tm=tm_s)
    qx_h = qx_s.reshape(n, t_new, HEADS // 2, 2, 2, SB_D)
    q_h = jnp.stack([qx_h[:, :, :, 0, 0], qx_h[:, :, :, 1, 1]], axis=3).reshape(n, t_new, HEADS, SB_D)
    q_h = jnp.transpose(q_h, (0, 2, 1, 3))
    pad_keys = lambda a: jnp.pad(a, ((0, 0), (0, 0), (0, KEY_BLOCK - t_new), (0, 0)))
    kn = pad_keys(jnp.transpose(kb_s.reshape(n, t_new, HEADS, SB_D), (0, 2, 1, 3)))
    vn = pad_keys(jnp.transpose(vb_s.reshape(n, t_new, HEADS, SB_D), (0, 2, 1, 3)))
    sb_os = _sb_sample(page_table, q_h, kn, vn, u2, cache_sb_k.reshape(cache_sb_k.shape[1:]),
                       cache_sb_v.reshape(cache_sb_v.shape[1:]))
    sb_os = jnp.transpose(sb_os, (0, 2, 1, 3)).reshape(ts, SB_W)
    q_m = jnp.transpose(qm_s.reshape(n, t_new, HEADS, 2 * LANES), (0, 2, 1, 3)).reshape(n, HEADS * t_new, 2 * LANES)
    kvn = jnp.pad(kv_s.reshape(n, t_new, 2 * LANES), ((0, 0), (0, KEY_BLOCK - t_new), (0, 0)))
    group = 8 if n_pages % 8 == 0 else 1
    lat_os = _mla_sample(page_table, q_m, kvn, cache_mla_ckv.reshape(cache_mla_ckv.shape[1:]),
                         cache_mla_kpe.reshape(cache_mla_kpe.shape[1:]), group)
    lat_os = jnp.transpose(lat_os.reshape(n, HEADS, t_new, KV_LORA), (0, 2, 1, 3)).reshape(ts, HEADS * KV_LORA)
    y_s = _post(sb_os, lat_os, xs, p_sample[0].reshape(ts, -1), *post_w, tm=tm_s)

    return (y_p.reshape(b, s, d), y_s.reshape(n, t_new, d),
            k_p.reshape(1, b, s, HEADS, SB_D), v_p.reshape(1, b, s, HEADS, SB_D),
            ckv_p.reshape(1, b, s, KV_LORA), kpe_p.reshape(1, b, s, ROPE),
            k_s.reshape(1, n, t_new, HEADS, SB_D), v_s.reshape(1, n, t_new, HEADS, SB_D),
            ckv_s.reshape(1, n, t_new, KV_LORA), kpe_s.reshape(1, n, t_new, ROPE))
```

```python
import functools

import jax
import jax.numpy as jnp
from jax import lax
from jax.experimental import pallas as pl
from jax.experimental.pallas import tpu as pltpu

F32 = jnp.float32
BF16 = jnp.bfloat16

LANES = 128
HEADS = 8
SB_D = 64
SB_W = HEADS * SB_D
NOPE = 64
ROPE = 32
Q_LORA = 256
KV_LORA = 128
V_DIM = 64
MLA_W = HEADS * V_DIM
FF_CHUNK = 256
ROPE_THETA = 10000.0
NORM_EPS = 1e-6
SB_SCALE = SB_D ** -0.5
MLA_SCALE = (NOPE + ROPE) ** -0.5
KEY_BLOCK = 128
SB_DEAD_LOG = 104.0
SB_SEQS_PER_STEP = 4
SB_FIRST_PAGES = 2
MLA_PAGES_PER_STEP = 16
VMEM_LIMIT = 56 * 1024 * 1024


def _rms(x, g):
    return x * lax.rsqrt(jnp.mean(x * x, axis=-1, keepdims=True) + NORM_EPS) * g


def _dot(a, b):
    return jnp.dot(a, b, preferred_element_type=F32)


def _dot_nt(a, b):
    return lax.dot_general(a, b, (((1,), (1,)), ((), ())), preferred_element_type=F32)


def _const_spec(shape):
    nd = len(shape)
    return pl.BlockSpec(shape, lambda *_: (0,) * nd)


def _pre_kernel(x_ref, cos_ref, sin_ref, ga_ref, gq_ref, gkv_ref, w1_ref, w2_ref, wuk_ref,
                qx_ref, k_ref, v_ref, kb_ref, vb_ref, ckv_ref, kpe_ref, kv_ref, qm_ref):
    x = x_ref[...]
    tm = x.shape[0]
    h = _rms(x, ga_ref[...]).astype(BF16)
    proj = _dot(h, w1_ref[...])
    lane = lax.broadcasted_iota(jnp.int32, (tm, LANES), 1)
    low = lane < SB_D
    for hp in range(HEADS // 2):
        blk = proj[:, hp * LANES:(hp + 1) * LANES] * SB_SCALE
        qx_ref[:, (2 * hp) * LANES:(2 * hp + 1) * LANES] = jnp.where(low, blk, 0.0).astype(BF16)
        qx_ref[:, (2 * hp + 1) * LANES:(2 * hp + 2) * LANES] = jnp.where(low, 0.0, blk).astype(BF16)
    k = proj[:, SB_W:2 * SB_W]
    v = proj[:, 2 * SB_W:3 * SB_W]
    k_ref[0] = k.T.reshape(HEADS, SB_D, tm)
    v_ref[0] = v.T.reshape(HEADS, SB_D, tm)
    kb_ref[...] = k.astype(BF16)
    vb_ref[...] = v.astype(BF16)
    o = 3 * SB_W
    cqn = _rms(proj[:, o:o + Q_LORA], gq_ref[...]).astype(BF16)
    o += Q_LORA
    ckv = _rms(proj[:, o:o + KV_LORA], gkv_ref[...])
    o += KV_LORA
    cos = cos_ref[...]
    sin = sin_ref[...]
    kpe = proj[:, o:o + LANES] * cos + proj[:, o + LANES:o + 2 * LANES] * sin
    ckv_ref[...] = ckv
    kpe_ref[0] = kpe.T[:ROPE]
    kv_ref[:, :KV_LORA] = ckv.astype(BF16)
    kv_ref[:, KV_LORA:] = kpe.astype(BF16)
    q2 = _dot(cqn, w2_ref[...])
    lat = _dot(q2[:, :HEADS * NOPE].astype(BF16), wuk_ref[...])
    o1 = HEADS * NOPE
    o2 = o1 + HEADS * LANES
    for hh in range(HEADS):
        pe = q2[:, o1 + hh * LANES:o1 + (hh + 1) * LANES] * cos + q2[:, o2 + hh * LANES:o2 + (hh + 1) * LANES] * sin
        qm_ref[:, hh * 2 * LANES:hh * 2 * LANES + LANES] = (lat[:, hh * LANES:(hh + 1) * LANES] * MLA_SCALE).astype(BF16)
        qm_ref[:, hh * 2 * LANES + LANES:(hh + 1) * 2 * LANES] = (pe * MLA_SCALE).astype(BF16)


def _pre(x, cos_t, sin_t, n_tab_blocks, groups, ga, gq, gkv, w1, w2, wuk, tm):
    t, d = x.shape
    grid = (t // tm,)
    per = t // groups
    nper = per // tm
    row = lambda w: pl.BlockSpec((tm, w), lambda i: (i, 0))
    tab = pl.BlockSpec((tm, LANES), lambda i: (i % n_tab_blocks, 0))
    head_t = pl.BlockSpec((1, HEADS, SB_D, tm), lambda i: (i // nper, 0, 0, i % nper))
    outs = [
        (jax.ShapeDtypeStruct((t, HEADS * LANES), BF16), row(HEADS * LANES)),
        (jax.ShapeDtypeStruct((groups, HEADS, SB_D, per), F32), head_t),
        (jax.ShapeDtypeStruct((groups, HEADS, SB_D, per), F32), head_t),
        (jax.ShapeDtypeStruct((t, SB_W), BF16), row(SB_W)),
        (jax.ShapeDtypeStruct((t, SB_W), BF16), row(SB_W)),
        (jax.ShapeDtypeStruct((t, KV_LORA), F32), row(KV_LORA)),
        (jax.ShapeDtypeStruct((groups, ROPE, per), F32),
         pl.BlockSpec((1, ROPE, tm), lambda i: (i // nper, 0, i % nper))),
        (jax.ShapeDtypeStruct((t, 2 * LANES), BF16), row(2 * LANES)),
        (jax.ShapeDtypeStruct((t, HEADS * 2 * LANES), BF16), row(HEADS * 2 * LANES)),
    ]
    return pl.pallas_call(
        _pre_kernel,
        grid=grid,
        in_specs=[row(d), tab, tab, _const_spec(ga.shape), _const_spec(gq.shape), _const_spec(gkv.shape),
                  _const_spec(w1.shape), _const_spec(w2.shape), _const_spec(wuk.shape)],
        out_specs=[o[1] for o in outs],
        out_shape=[o[0] for o in outs],
        compiler_params=pltpu.CompilerParams(dimension_semantics=("arbitrary",), vmem_limit_bytes=VMEM_LIMIT),
        name="pre",
    )(x, cos_t, sin_t, ga, gq, gkv, w1, w2, wuk)


def _sb_blocks(zs, masks, u2, carry):
    ws = []
    for z, mask in zip(zs, masks):
        l1p = jnp.log1p(jnp.exp(-jnp.abs(z)))
        log_beta = jnp.minimum(z, 0.0) - l1p
        log_keep = jnp.minimum(-z, 0.0) - l1p
        if mask is not None:
            log_keep = jnp.where(mask, log_keep, 0.0)
        hi = log_keep.astype(BF16)
        lo = (log_keep - hi.astype(F32)).astype(BF16)
        cs = _dot(hi, u2) + _dot(lo, u2)
        w = jnp.exp(log_beta + cs[:, :LANES] + carry)
        ws.append(w if mask is None else jnp.where(mask, w, 0.0))
        carry = carry + cs[:, LANES:]
    return ws, carry


def _sb_prompt_kernel(q_ref, k_ref, v_ref, u2_ref, o_ref, acc_ref, car_ref, *, tq):
    qi = pl.program_id(1)
    rows = HEADS * tq
    lane = lax.broadcasted_iota(jnp.int32, (rows, LANES), 1)
    row = lax.broadcasted_iota(jnp.int32, (rows, LANES), 0) & (tq - 1)
    acc_ref[...] = jnp.zeros_like(acc_ref)
    car_ref[...] = jnp.zeros_like(car_ref)
    u2 = u2_ref[...]

    def body(state):
        j, _ = state
        mask = lane < row + (qi - j) * KEY_BLOCK
        start = pl.multiple_of(j * KEY_BLOCK, KEY_BLOCK)
        zs = []
        for hh in range(HEADS):
            hp = hh // 2
            kblk = k_ref[0, pl.ds(start, KEY_BLOCK), hp * LANES:(hp + 1) * LANES]
            zs.append(_dot_nt(q_ref[0, :, hh * LANES:(hh + 1) * LANES], kblk))
        (w,), car = _sb_blocks([jnp.concatenate(zs, axis=0)], [mask], u2, car_ref[...])
        car_ref[...] = car
        for hh in range(HEADS):
            hp = hh // 2
            vblk = v_ref[0, pl.ds(start, KEY_BLOCK), hp * LANES:(hp + 1) * LANES]
            acc_ref[hh] += _dot(w[hh * tq:(hh + 1) * tq].astype(BF16), vblk)
        return j - 1, jnp.max(car) < -SB_DEAD_LOG

    lax.while_loop(lambda s: (s[0] >= 0) & jnp.logical_not(s[1]), body, (qi, False))
    low = lax.broadcasted_iota(jnp.int32, (tq, LANES), 1) < SB_D
    for hp in range(HEADS // 2):
        o_ref[0, :, hp * LANES:(hp + 1) * LANES] = jnp.where(low, acc_ref[2 * hp], acc_ref[2 * hp + 1])


def _sb_prompt(qx, kb, vb, u2, tq):
    b, s, _ = qx.shape
    return pl.pallas_call(
        functools.partial(_sb_prompt_kernel, tq=tq),
        grid=(b, s // tq),
        in_specs=[pl.BlockSpec((1, tq, HEADS * LANES), lambda bi, qi: (bi, qi, 0)),
                  pl.BlockSpec((1, s, SB_W), lambda bi, qi: (bi, 0, 0)),
                  pl.BlockSpec((1, s, SB_W), lambda bi, qi: (bi, 0, 0)),
                  _const_spec(u2.shape)],
        out_specs=pl.BlockSpec((1, tq, SB_W), lambda bi, qi: (bi, qi, 0)),
        out_shape=jax.ShapeDtypeStruct((b, s, SB_W), F32),
        scratch_shapes=[pltpu.VMEM((HEADS, tq, LANES), F32), pltpu.VMEM((HEADS * tq, LANES), F32)],
        compiler_params=pltpu.CompilerParams(dimension_semantics=("arbitrary", "arbitrary"),
                                             vmem_limit_bytes=VMEM_LIMIT),
        name="sb_prompt",
    )(qx, kb, vb, u2)


def _softmax_step(s, kv_lat, m_ref, l_ref, acc_ref):
    m_prev = m_ref[...]
    m_new = jnp.maximum(m_prev, jnp.max(s, axis=-1, keepdims=True))
    p = jnp.exp(s - m_new)
    alpha = jnp.exp(m_prev - m_new)
    l_ref[...] = alpha * l_ref[...] + jnp.sum(p, axis=-1, keepdims=True)
    acc_ref[...] = alpha * acc_ref[...] + _dot(p.astype(BF16), kv_lat)
    m_ref[...] = m_new


def _mla_prompt_kernel(qm_ref, kv_ref, o_ref, qs_ref, m_ref, l_ref, acc_ref, *, tq):
    qi = pl.program_id(1)
    rows = HEADS * tq
    for hh in range(HEADS):
        qs_ref[hh * tq:(hh + 1) * tq, :] = qm_ref[0, :, hh * 2 * LANES:(hh + 1) * 2 * LANES]
    m_ref[...] = jnp.full_like(m_ref, -jnp.inf)
    l_ref[...] = jnp.zeros_like(l_ref)
    acc_ref[...] = jnp.zeros_like(acc_ref)
    col = lax.broadcasted_iota(jnp.int32, (rows, LANES), 1)
    row = lax.broadcasted_iota(jnp.int32, (rows, LANES), 0) & (tq - 1)

    def body(j, carry):
        start = pl.multiple_of(j * KEY_BLOCK, KEY_BLOCK)
        kv = kv_ref[0, pl.ds(start, KEY_BLOCK), :]
        s = _dot_nt(qs_ref[...], kv)
        s = jnp.where(col <= row + (qi - j) * KEY_BLOCK, s, -jnp.inf)
        _softmax_step(s, kv[:, :KV_LORA], m_ref, l_ref, acc_ref)
        return carry

    lax.fori_loop(0, qi + 1, body, 0)
    out = acc_ref[...] / l_ref[...]
    for hh in range(HEADS):
        o_ref[0, :, hh * LANES:(hh + 1) * LANES] = out[hh * tq:(hh + 1) * tq]


def _mla_prompt(qm, kv, tq):
    b, s, _ = qm.shape
    rows = HEADS * tq
    return pl.pallas_call(
        functools.partial(_mla_prompt_kernel, tq=tq),
        grid=(b, s // tq),
        in_specs=[pl.BlockSpec((1, tq, HEADS * 2 * LANES), lambda bi, qi: (bi, qi, 0)),
                  pl.BlockSpec((1, s, 2 * LANES), lambda bi, qi: (bi, 0, 0))],
        out_specs=pl.BlockSpec((1, tq, HEADS * LANES), lambda bi, qi: (bi, qi, 0)),
        out_shape=jax.ShapeDtypeStruct((b, s, HEADS * LANES), F32),
        scratch_shapes=[pltpu.VMEM((rows, 2 * LANES), BF16), pltpu.VMEM((rows, LANES), F32),
                        pltpu.VMEM((rows, LANES), F32), pltpu.VMEM((rows, LANES), F32)],
        compiler_params=pltpu.CompilerParams(dimension_semantics=("arbitrary", "arbitrary"),
                                             vmem_limit_bytes=VMEM_LIMIT),
        name="mla_prompt",
    )(qm, kv)


def _sb_sample_kernel(pt_ref, q_ref, kn_ref, vn_ref, u2_ref, ck_ref, cv_ref, o_ref,
                      kbuf, vbuf, xk, xv, sem, xsem, acc_ref, car_ref, *, n_pages, t_new, seqs):
    step = pl.program_id(0)
    rows = HEADS * t_new
    u2 = u2_ref[...]
    lane = lax.broadcasted_iota(jnp.int32, (rows, LANES), 1)
    tq = lax.broadcasted_iota(jnp.int32, (rows, LANES), 0) & (t_new - 1)
    out_head = lax.broadcasted_iota(jnp.int32, (t_new, SB_W), 1) // SB_D

    def first_copies(sq):
        cps = []
        for b in range(SB_FIRST_PAGES):
            page = pt_ref[step * seqs + sq, n_pages - 1 - b]
            cps.append(pltpu.make_async_copy(ck_ref.at[page], kbuf.at[sq, b], sem.at[sq, 0]))
            cps.append(pltpu.make_async_copy(cv_ref.at[page], vbuf.at[sq, b], sem.at[sq, 1]))
        return cps

    for sq in range(seqs):
        for cp in first_copies(sq):
            cp.start()

    for sq in range(seqs):
        n = step * seqs + sq
        q = q_ref[sq]
        for cp in first_copies(sq):
            cp.wait()
        zs = [_dot(q, kn_ref[sq])] + [_dot(q, kbuf[sq, b].astype(BF16)) for b in range(SB_FIRST_PAGES)]
        ws, car = _sb_blocks(zs, [lane < tq] + [None] * SB_FIRST_PAGES, u2, jnp.zeros((rows, LANES), F32))
        acc = _dot_nt(ws[0].astype(BF16), vn_ref[sq])
        for b in range(SB_FIRST_PAGES):
            acc += _dot_nt(ws[1 + b].astype(BF16), vbuf[sq, b].astype(BF16))
        acc_ref[...] = acc
        car_ref[...] = car

        def more(state):
            i, _ = state
            page = pt_ref[n, n_pages - 1 - i]
            ck = pltpu.make_async_copy(ck_ref.at[page], xk, xsem.at[0])
            cv = pltpu.make_async_copy(cv_ref.at[page], xv, xsem.at[1])
            ck.start()
            cv.start()
            ck.wait()
            cv.wait()
            (w,), c = _sb_blocks([_dot(q, xk[...].astype(BF16))], [None], u2, car_ref[...])
            acc_ref[...] += _dot_nt(w.astype(BF16), xv[...].astype(BF16))
            car_ref[...] = c
            return i + 1, jnp.max(c) < -SB_DEAD_LOG

        lax.while_loop(lambda s: (s[0] < n_pages) & jnp.logical_not(s[1]), more,
                       (SB_FIRST_PAGES, jnp.max(car) < -SB_DEAD_LOG))
        a = acc_ref[...]
        out = jnp.zeros((t_new, SB_W), F32)
        for hh in range(HEADS):
            out = jnp.where(out_head == hh, a[hh * t_new:(hh + 1) * t_new], out)
        o_ref[sq] = out


def _sb_sample(page_table, qbd, kn, vn, u2, cache_k, cache_v, seqs):
    n, rows, _ = qbd.shape
    t_new = rows // HEADS
    n_pages = page_table.shape[1]
    assert n % seqs == 0 and n_pages >= SB_FIRST_PAGES and cache_k.shape[1:] == (SB_W, KEY_BLOCK)
    per_seq = lambda shape: pl.BlockSpec((seqs,) + shape, lambda i, pt: (i, 0, 0))
    grid_spec = pltpu.PrefetchScalarGridSpec(
        num_scalar_prefetch=1,
        grid=(n // seqs,),
        in_specs=[per_seq((rows, SB_W)), per_seq((SB_W, KEY_BLOCK)), per_seq((SB_W, KEY_BLOCK)),
                  pl.BlockSpec(u2.shape, lambda i, pt: (0, 0)),
                  pl.BlockSpec(memory_space=pl.ANY),
                  pl.BlockSpec(memory_space=pl.ANY)],
        out_specs=per_seq((t_new, SB_W)),
        scratch_shapes=[pltpu.VMEM((seqs, SB_FIRST_PAGES, SB_W, KEY_BLOCK), F32),
                        pltpu.VMEM((seqs, SB_FIRST_PAGES, SB_W, KEY_BLOCK), F32),
                        pltpu.VMEM((SB_W, KEY_BLOCK), F32),
                        pltpu.VMEM((SB_W, KEY_BLOCK), F32),
                        pltpu.SemaphoreType.DMA((seqs, 2)),
                        pltpu.SemaphoreType.DMA((2,)),
                        pltpu.VMEM((rows, SB_W), F32),
                        pltpu.VMEM((rows, LANES), F32)],
    )
    return pl.pallas_call(
        functools.partial(_sb_sample_kernel, n_pages=n_pages, t_new=t_new, seqs=seqs),
        grid_spec=grid_spec,
        out_shape=jax.ShapeDtypeStruct((n, t_new, SB_W), F32),
        compiler_params=pltpu.CompilerParams(dimension_semantics=("arbitrary",), vmem_limit_bytes=VMEM_LIMIT),
        name="sb_sample",
    )(page_table, qbd, kn, vn, u2, cache_k, cache_v)


def _mla_sample_kernel(pt_ref, q_ref, kvn_ref, *refs, group, t_new):
    ckv_refs = refs[:group]
    kpe_refs = refs[group:2 * group]
    o_ref, m_ref, l_ref, acc_ref = refs[2 * group:]
    j = pl.program_id(1)
    rows = HEADS * t_new
    q = q_ref[0]

    @pl.when(j == 0)
    def _():
        m_ref[...] = jnp.full_like(m_ref, -jnp.inf)
        l_ref[...] = jnp.zeros_like(l_ref)
        acc_ref[...] = jnp.zeros_like(acc_ref)
        col = lax.broadcasted_iota(jnp.int32, (rows, LANES), 1)
        tq = lax.broadcasted_iota(jnp.int32, (rows, LANES), 0) & (t_new - 1)
        kvn = kvn_ref[0]
        s = jnp.where(col <= tq, _dot_nt(q, kvn), -jnp.inf)
        _softmax_step(s, kvn[:, :KV_LORA], m_ref, l_ref, acc_ref)

    q_lat = q[:, :KV_LORA]
    q_pe = q[:, KV_LORA:KV_LORA + ROPE]
    cks = [ckv_refs[g][...].astype(BF16) for g in range(group)]
    ss = [_dot_nt(q_lat, cks[g]) + _dot(q_pe, kpe_refs[g][...].astype(BF16)) for g in range(group)]
    smax = ss[0]
    for g in range(1, group):
        smax = jnp.maximum(smax, ss[g])
    m_prev = m_ref[...]
    m_new = jnp.maximum(m_prev, jnp.max(smax, axis=-1, keepdims=True))
    alpha = jnp.exp(m_prev - m_new)
    psum = None
    pv = None
    for g in range(group):
        p = jnp.exp(ss[g] - m_new)
        psum = p if psum is None else psum + p
        d = _dot(p.astype(BF16), cks[g])
        pv = d if pv is None else pv + d
    l_ref[...] = alpha * l_ref[...] + jnp.sum(psum, axis=-1, keepdims=True)
    acc_ref[...] = alpha * acc_ref[...] + pv
    m_ref[...] = m_new

    @pl.when(j == pl.num_programs(1) - 1)
    def _():
        o_ref[0] = acc_ref[...] / l_ref[...]


def _mla_sample(page_table, q, kvn, cache_ckv, cache_kpe, group):
    n, rows, _ = q.shape
    t_new = rows // HEADS
    n_pages = page_table.shape[1]
    assert n_pages % group == 0 and cache_ckv.shape[1:] == (KEY_BLOCK, KV_LORA)
    assert cache_kpe.shape[1:] == (ROPE, KEY_BLOCK)

    def page_spec(shape, g):
        return pl.BlockSpec((None,) + shape, lambda i, j, pt: (pt[i, j * group + g], 0, 0))

    grid_spec = pltpu.PrefetchScalarGridSpec(
        num_scalar_prefetch=1,
        grid=(n, n_pages // group),
        in_specs=[pl.BlockSpec((1, rows, 2 * LANES), lambda i, j, pt: (i, 0, 0)),
                  pl.BlockSpec((1, KEY_BLOCK, 2 * LANES), lambda i, j, pt: (i, 0, 0))]
                 + [page_spec((KEY_BLOCK, KV_LORA), g) for g in range(group)]
                 + [page_spec((ROPE, KEY_BLOCK), g) for g in range(group)],
        out_specs=pl.BlockSpec((1, rows, KV_LORA), lambda i, j, pt: (i, 0, 0)),
        scratch_shapes=[pltpu.VMEM((rows, LANES), F32), pltpu.VMEM((rows, LANES), F32),
                        pltpu.VMEM((rows, KV_LORA), F32)],
    )
    return pl.pallas_call(
        functools.partial(_mla_sample_kernel, group=group, t_new=t_new),
        grid_spec=grid_spec,
        out_shape=jax.ShapeDtypeStruct((n, rows, KV_LORA), F32),
        compiler_params=pltpu.CompilerParams(dimension_semantics=("arbitrary", "arbitrary"),
                                             vmem_limit_bytes=VMEM_LIMIT),
        name="mla_sample",
    )(page_table, q, kvn, *([cache_ckv] * group), *([cache_kpe] * group))


def _post_kernel(sb_ref, lat_ref, x_ref, p_ref, wuv_ref, gsb_ref, gmla_ref, wo_ref, gffn_ref, wg_ref, wu_ref,
                 wd_ref, gple_ref, wpg_ref, wpp_ref, gfin_ref, y_ref, mrg_ref, ff_ref, *, n_chunks):
    mla_o = _dot(lat_ref[...].astype(BF16), wuv_ref[...])
    mrg_ref[:, :SB_W] = _rms(sb_ref[...], gsb_ref[...]).astype(BF16)
    mrg_ref[:, SB_W:] = _rms(mla_o, gmla_ref[...]).astype(BF16)
    x1 = x_ref[...] + _dot(mrg_ref[...], wo_ref[...])
    h = _rms(x1, gffn_ref[...]).astype(BF16)
    ff_ref[...] = jnp.zeros_like(ff_ref)

    def chunk(c, carry):
        g = _dot(h, wg_ref[c])
        u = _dot(h, wu_ref[c])
        a = (g * jax.nn.sigmoid(g) * u).astype(BF16)
        ff_ref[...] += _dot(a, wd_ref[c])
        return carry

    lax.fori_loop(0, n_chunks, chunk, 0)
    x2 = x1 + ff_ref[...]
    gate = jax.nn.sigmoid(_dot(_rms(x2, gple_ref[...]).astype(BF16), wpg_ref[...]))
    x3 = x2 + gate * _dot(p_ref[...].astype(BF16), wpp_ref[...])
    y_ref[...] = _rms(x3, gfin_ref[...])


def _post(sb_o, lat_o, x, p, wuv, gsb, gmla, wo, gffn, wg, wu, wd, gple, wpg, wpp, gfin, tm):
    t, d = x.shape
    row = lambda w: pl.BlockSpec((tm, w), lambda i: (i, 0))

    def resident(a):
        nd = a.ndim
        return pl.BlockSpec(a.shape, lambda i: (0,) * nd, pipeline_mode=pl.Buffered(1))

    weights = (wuv, gsb, gmla, wo, gffn, wg, wu, wd, gple, wpg, wpp, gfin)
    return pl.pallas_call(
        functools.partial(_post_kernel, n_chunks=wg.shape[0]),
        grid=(t // tm,),
        in_specs=[row(SB_W), row(HEADS * LANES), row(d), row(p.shape[1])] + [resident(a) for a in weights],
        out_specs=row(d),
        out_shape=jax.ShapeDtypeStruct((t, d), F32),
        scratch_shapes=[pltpu.VMEM((tm, SB_W + MLA_W), BF16), pltpu.VMEM((tm, d), F32)],
        compiler_params=pltpu.CompilerParams(dimension_semantics=("arbitrary",), vmem_limit_bytes=VMEM_LIMIT),
        name="post",
    )(sb_o, lat_o, x, p, *weights)


def _rope_tables(pos):
    half = ROPE // 2
    inv = ROPE_THETA ** (-2.0 * jnp.arange(half, dtype=F32) / ROPE)
    ang = pos.astype(F32)[:, None] * inv[None, :]
    cos, sin = jnp.cos(ang), jnp.sin(ang)
    pad = jnp.zeros((pos.shape[0], LANES - ROPE), F32)
    return (jnp.concatenate([cos, cos, pad], axis=1), jnp.concatenate([-sin, sin, pad], axis=1))


def _swap_halves(w):
    half = ROPE // 2
    return jnp.concatenate([w[..., half:], w[..., :half]], axis=-1)


def _place(w):
    return jnp.pad(w, [(0, 0)] * (w.ndim - 1) + [(0, LANES - ROPE)])


def _block_diag(w):
    hh, a, b = w.shape
    eye = jnp.eye(hh, dtype=w.dtype)
    return (eye[:, None, :, None] * w[:, :, None, :]).reshape(hh * a, hh * b)


def _token_tile(t):
    for tm in (512, 256, 128):
        if t % tm == 0:
            return tm
    raise ValueError(f"token count {t} is not a multiple of 128")


def _largest_divisor(n, cap):
    return max(g for g in range(1, cap + 1) if n % g == 0)


def kernel(x_prompt, x_sample, cache_sb_k, cache_sb_v, cache_mla_ckv, cache_mla_kpe, page_table, p_prompt, p_sample, norm_attn, w_in, norm_q_a, w_uq, norm_kv_a, w_uk, w_uv, norm_sb_out, norm_mla_out, w_o, norm_ffn, w_gate, w_up, w_down, norm_ple, w_ple_gate, w_ple_proj, norm_final):
    depth = w_in.shape[0]
    assert depth == 1, "single trunk layer"
    b, s, d = x_prompt.shape
    n, t_new, _ = x_sample.shape
    n_phys = cache_sb_k.shape[1]
    n_pages = page_table.shape[1]
    past = n_pages * cache_sb_k.shape[2]
    assert s % KEY_BLOCK == 0 and t_new == 8 and cache_sb_k.shape[2] == KEY_BLOCK

    row2 = lambda g: g.reshape(1, -1).astype(F32)
    wi = w_in[0]
    o_kpe = 3 * SB_W + Q_LORA + KV_LORA
    w_kpe = wi[:, o_kpe:o_kpe + ROPE]
    w1 = jnp.concatenate([wi[:, :o_kpe], _place(w_kpe), _place(_swap_halves(w_kpe))], axis=1).astype(BF16)
    wq = w_uq[0].reshape(Q_LORA, HEADS, NOPE + ROPE)
    wq_pe = wq[:, :, NOPE:]
    w2 = jnp.concatenate([wq[:, :, :NOPE].reshape(Q_LORA, HEADS * NOPE),
                          _place(wq_pe).reshape(Q_LORA, HEADS * LANES),
                          _place(_swap_halves(wq_pe)).reshape(Q_LORA, HEADS * LANES)], axis=1).astype(BF16)
    wuk = _block_diag(jnp.transpose(w_uk[0], (1, 2, 0))).astype(BF16)
    wuv = _block_diag(jnp.transpose(w_uv[0], (1, 0, 2))).astype(BF16)
    n_chunks = w_gate.shape[2] // FF_CHUNK
    wg = jnp.transpose(w_gate[0].reshape(d, n_chunks, FF_CHUNK), (1, 0, 2)).astype(BF16)
    wu = jnp.transpose(w_up[0].reshape(d, n_chunks, FF_CHUNK), (1, 0, 2)).astype(BF16)
    wd = w_down[0].reshape(n_chunks, FF_CHUNK, d).astype(BF16)
    post_w = (wuv, row2(norm_sb_out[0]), row2(norm_mla_out[0]), w_o[0].astype(BF16), row2(norm_ffn[0]), wg, wu, wd,
              row2(norm_ple[0]), w_ple_gate[0].astype(BF16), w_ple_proj[0].astype(BF16), row2(norm_final))
    pre_w = (row2(norm_attn[0]), row2(norm_q_a[0]), row2(norm_kv_a[0]), w1, w2, wuk)
    ki = lax.broadcasted_iota(jnp.int32, (KEY_BLOCK, 2 * LANES), 0)
    si = lax.broadcasted_iota(jnp.int32, (KEY_BLOCK, 2 * LANES), 1)
    u2 = ((ki > si) | (si >= LANES)).astype(BF16)

    tm_p = _token_tile(s)
    cos_p, sin_p = _rope_tables(jnp.arange(s, dtype=jnp.int32))
    xp = x_prompt.reshape(b * s, d)
    qx, k_p, v_p, kb, vb, ckv_p, kpe_p, kv_p, qm = _pre(xp, cos_p, sin_p, s // tm_p, b, *pre_w, tm=tm_p)
    sb_o = _sb_prompt(qx.reshape(b, s, -1), kb.reshape(b, s, -1), vb.reshape(b, s, -1), u2, tq=KEY_BLOCK)
    lat_o = _mla_prompt(qm.reshape(b, s, -1), kv_p.reshape(b, s, -1), tq=KEY_BLOCK)
    y_p = _post(sb_o.reshape(b * s, -1), lat_o.reshape(b * s, -1), xp, p_prompt[0].reshape(b * s, -1), *post_w,
                tm=tm_p)

    ts = n * t_new
    tm_s = _token_tile(ts)
    cos_s, sin_s = _rope_tables(past + (jnp.arange(tm_s, dtype=jnp.int32) % t_new))
    xs = x_sample.reshape(ts, d)
    qx_s, k_s, v_s, _, _, ckv_s, kpe_s, kv_s, qm_s = _pre(xs, cos_s, sin_s, 1, 1, *pre_w, tm=tm_s)
    q_s = qx_s.reshape(ts, HEADS // 2, 2, LANES).sum(axis=2)
    q_s = jnp.transpose(q_s.reshape(n, t_new, HEADS, SB_D), (0, 2, 1, 3))
    eye = jnp.eye(HEADS, dtype=q_s.dtype)
    qbd = (q_s[:, :, :, None, :] * eye[None, :, None, :, None]).reshape(n, HEADS * t_new, SB_W)

    def new_keys_t(a):
        a = jnp.transpose(a.reshape(HEADS, SB_D, n, t_new), (2, 0, 1, 3)).reshape(n, SB_W, t_new)
        return jnp.pad(a, ((0, 0), (0, 0), (0, KEY_BLOCK - t_new))).astype(BF16)

    ck = jnp.transpose(cache_sb_k.reshape(n_phys, KEY_BLOCK, HEADS, SB_D), (0, 2, 3, 1)).reshape(n_phys, SB_W, KEY_BLOCK)
    cv = jnp.transpose(cache_sb_v.reshape(n_phys, KEY_BLOCK, HEADS, SB_D), (0, 2, 3, 1)).reshape(n_phys, SB_W, KEY_BLOCK)
    sb_os = _sb_sample(page_table, qbd, new_keys_t(k_s), new_keys_t(v_s), u2, ck, cv,
                       seqs=_largest_divisor(n, SB_SEQS_PER_STEP))
    q_m = jnp.transpose(qm_s.reshape(n, t_new, HEADS, 2 * LANES), (0, 2, 1, 3)).reshape(n, HEADS * t_new, 2 * LANES)
    kvn = jnp.pad(kv_s.reshape(n, t_new, 2 * LANES), ((0, 0), (0, KEY_BLOCK - t_new), (0, 0)))
    c_ckv = cache_mla_ckv.reshape(n_phys, KEY_BLOCK, KV_LORA)
    c_kpe = jnp.transpose(cache_mla_kpe.reshape(n_phys, KEY_BLOCK, ROPE), (0, 2, 1))
    lat_os = _mla_sample(page_table, q_m, kvn, c_ckv, c_kpe, _largest_divisor(n_pages, MLA_PAGES_PER_STEP))
    lat_os = jnp.transpose(lat_os.reshape(n, HEADS, t_new, KV_LORA), (0, 2, 1, 3)).reshape(ts, HEADS * KV_LORA)
    y_s = _post(sb_os.reshape(ts, SB_W), lat_os, xs, p_sample[0].reshape(ts, -1), *post_w, tm=tm_s)

    def heads_last(a, lead):
        return jnp.transpose(a, (0, 3, 1, 2)).reshape((1,) + lead + (HEADS, SB_D))

    def rope_last(a, lead):
        return jnp.transpose(a, (0, 2, 1)).reshape((1,) + lead + (ROPE,))

    return (y_p.reshape(b, s, d), y_s.reshape(n, t_new, d),
            heads_last(k_p, (b, s)), heads_last(v_p, (b, s)),
            ckv_p.reshape(1, b, s, KV_LORA), rope_last(kpe_p, (b, s)),
            heads_last(k_s, (n, t_new)), heads_last(v_s, (n, t_new)),
            ckv_s.reshape(1, n, t_new, KV_LORA), rope_last(kpe_s, (n, t_new)))
```

```python
import functools

import jax
import jax.numpy as jnp
from jax import lax
from jax.experimental import pallas as pl
from jax.experimental.pallas import tpu as pltpu

F32 = jnp.float32
BF16 = jnp.bfloat16

LANES = 128
HEADS = 8
SB_D = 64
SB_W = HEADS * SB_D
NOPE = 64
ROPE = 32
Q_LORA = 256
KV_LORA = 128
V_DIM = 64
MLA_W = HEADS * V_DIM
FF_CHUNK = 256
ROPE_THETA = 10000.0
NORM_EPS = 1e-6
SB_SCALE = SB_D ** -0.5
MLA_SCALE = (NOPE + ROPE) ** -0.5
KEY_BLOCK = 128
SB_DEAD_LOG = 104.0
SB_SEQS_PER_STEP = 4
SB_FIRST_PAGES = 2
MLA_PAGES_PER_STEP = 32
MLA_PROMPT_KEYS = 256
VMEM_LIMIT = 56 * 1024 * 1024


def _rms(x, g):
    return x * lax.rsqrt(jnp.mean(x * x, axis=-1, keepdims=True) + NORM_EPS) * g


def _dot(a, b):
    return jnp.dot(a, b, preferred_element_type=F32)


def _dot_nt(a, b):
    return lax.dot_general(a, b, (((1,), (1,)), ((), ())), preferred_element_type=F32)


def _const_spec(shape):
    nd = len(shape)
    return pl.BlockSpec(shape, lambda *_: (0,) * nd)


def _pre_kernel(x_ref, cos_ref, sin_ref, ga_ref, gq_ref, gkv_ref, w1_ref, w2_ref, wuk_ref,
                qx_ref, k_ref, v_ref, kb_ref, vb_ref, ckv_ref, kpe_ref, kv_ref, qm_ref):
    x = x_ref[...]
    tm = x.shape[0]
    h = _rms(x, ga_ref[...]).astype(BF16)
    proj = _dot(h, w1_ref[...])
    lane = lax.broadcasted_iota(jnp.int32, (tm, LANES), 1)
    low = lane < SB_D
    for hp in range(HEADS // 2):
        blk = proj[:, hp * LANES:(hp + 1) * LANES] * SB_SCALE
        qx_ref[:, (2 * hp) * LANES:(2 * hp + 1) * LANES] = jnp.where(low, blk, 0.0).astype(BF16)
        qx_ref[:, (2 * hp + 1) * LANES:(2 * hp + 2) * LANES] = jnp.where(low, 0.0, blk).astype(BF16)
    k = proj[:, SB_W:2 * SB_W]
    v = proj[:, 2 * SB_W:3 * SB_W]
    k_ref[0] = k.T.reshape(HEADS, SB_D, tm)
    v_ref[0] = v.T.reshape(HEADS, SB_D, tm)
    kb_ref[...] = k.astype(BF16)
    vb_ref[...] = v.astype(BF16)
    o = 3 * SB_W
    cqn = _rms(proj[:, o:o + Q_LORA], gq_ref[...]).astype(BF16)
    o += Q_LORA
    ckv = _rms(proj[:, o:o + KV_LORA], gkv_ref[...])
    o += KV_LORA
    cos = cos_ref[...]
    sin = sin_ref[...]
    kpe = proj[:, o:o + LANES] * cos + proj[:, o + LANES:o + 2 * LANES] * sin
    ckv_ref[...] = ckv
    kpe_ref[0] = kpe.T[:ROPE]
    kv_ref[:, :KV_LORA] = ckv.astype(BF16)
    kv_ref[:, KV_LORA:] = kpe.astype(BF16)
    q2 = _dot(cqn, w2_ref[...])
    lat = _dot(q2[:, :HEADS * NOPE].astype(BF16), wuk_ref[...])
    o1 = HEADS * NOPE
    o2 = o1 + HEADS * LANES
    for hh in range(HEADS):
        pe = q2[:, o1 + hh * LANES:o1 + (hh + 1) * LANES] * cos + q2[:, o2 + hh * LANES:o2 + (hh + 1) * LANES] * sin
        qm_ref[:, hh * 2 * LANES:hh * 2 * LANES + LANES] = (lat[:, hh * LANES:(hh + 1) * LANES] * MLA_SCALE).astype(BF16)
        qm_ref[:, hh * 2 * LANES + LANES:(hh + 1) * 2 * LANES] = (pe * MLA_SCALE).astype(BF16)


def _pre(x, cos_t, sin_t, n_tab_blocks, groups, ga, gq, gkv, w1, w2, wuk, tm):
    t, d = x.shape
    grid = (t // tm,)
    per = t // groups
    nper = per // tm
    row = lambda w: pl.BlockSpec((tm, w), lambda i: (i, 0))
    tab = pl.BlockSpec((tm, LANES), lambda i: (i % n_tab_blocks, 0))
    head_t = pl.BlockSpec((1, HEADS, SB_D, tm), lambda i: (i // nper, 0, 0, i % nper))
    outs = [
        (jax.ShapeDtypeStruct((t, HEADS * LANES), BF16), row(HEADS * LANES)),
        (jax.ShapeDtypeStruct((groups, HEADS, SB_D, per), F32), head_t),
        (jax.ShapeDtypeStruct((groups, HEADS, SB_D, per), F32), head_t),
        (jax.ShapeDtypeStruct((t, SB_W), BF16), row(SB_W)),
        (jax.ShapeDtypeStruct((t, SB_W), BF16), row(SB_W)),
        (jax.ShapeDtypeStruct((t, KV_LORA), F32), row(KV_LORA)),
        (jax.ShapeDtypeStruct((groups, ROPE, per), F32),
         pl.BlockSpec((1, ROPE, tm), lambda i: (i // nper, 0, i % nper))),
        (jax.ShapeDtypeStruct((t, 2 * LANES), BF16), row(2 * LANES)),
        (jax.ShapeDtypeStruct((t, HEADS * 2 * LANES), BF16), row(HEADS * 2 * LANES)),
    ]
    return pl.pallas_call(
        _pre_kernel,
        grid=grid,
        in_specs=[row(d), tab, tab, _const_spec(ga.shape), _const_spec(gq.shape), _const_spec(gkv.shape),
                  _const_spec(w1.shape), _const_spec(w2.shape), _const_spec(wuk.shape)],
        out_specs=[o[1] for o in outs],
        out_shape=[o[0] for o in outs],
        compiler_params=pltpu.CompilerParams(dimension_semantics=("arbitrary",), vmem_limit_bytes=VMEM_LIMIT),
        name="pre",
    )(x, cos_t, sin_t, ga, gq, gkv, w1, w2, wuk)


def _sb_blocks(zs, masks, u2, carry):
    ws = []
    for z, mask in zip(zs, masks):
        l1p = jnp.log1p(jnp.exp(-jnp.abs(z)))
        log_beta = jnp.minimum(z, 0.0) - l1p
        log_keep = jnp.minimum(-z, 0.0) - l1p
        if mask is not None:
            log_keep = jnp.where(mask, log_keep, 0.0)
        hi = log_keep.astype(BF16)
        lo = (log_keep - hi.astype(F32)).astype(BF16)
        cs = _dot(hi, u2) + _dot(lo, u2)
        w = jnp.exp(log_beta + cs[:, :LANES] + carry)
        ws.append(w if mask is None else jnp.where(mask, w, 0.0))
        carry = carry + cs[:, LANES:]
    return ws, carry


def _sb_prompt_kernel(q_ref, k_ref, v_ref, u2_ref, o_ref, acc_ref, car_ref, *, tq):
    qi = pl.program_id(1)
    rows = HEADS * tq
    lane = lax.broadcasted_iota(jnp.int32, (rows, LANES), 1)
    row = lax.broadcasted_iota(jnp.int32, (rows, LANES), 0) & (tq - 1)
    acc_ref[...] = jnp.zeros_like(acc_ref)
    car_ref[...] = jnp.zeros_like(car_ref)
    u2 = u2_ref[...]

    def body(state):
        j, _ = state
        mask = lane < row + (qi - j) * KEY_BLOCK
        start = pl.multiple_of(j * KEY_BLOCK, KEY_BLOCK)
        zs = []
        for hh in range(HEADS):
            hp = hh // 2
            kblk = k_ref[0, pl.ds(start, KEY_BLOCK), hp * LANES:(hp + 1) * LANES]
            zs.append(_dot_nt(q_ref[0, :, hh * LANES:(hh + 1) * LANES], kblk))
        (w,), car = _sb_blocks([jnp.concatenate(zs, axis=0)], [mask], u2, car_ref[...])
        car_ref[...] = car
        for hh in range(HEADS):
            hp = hh // 2
            vblk = v_ref[0, pl.ds(start, KEY_BLOCK), hp * LANES:(hp + 1) * LANES]
            acc_ref[hh] += _dot(w[hh * tq:(hh + 1) * tq].astype(BF16), vblk)
        return j - 1, jnp.max(car) < -SB_DEAD_LOG

    lax.while_loop(lambda s: (s[0] >= 0) & jnp.logical_not(s[1]), body, (qi, False))
    low = lax.broadcasted_iota(jnp.int32, (tq, LANES), 1) < SB_D
    for hp in range(HEADS // 2):
        o_ref[0, :, hp * LANES:(hp + 1) * LANES] = jnp.where(low, acc_ref[2 * hp], acc_ref[2 * hp + 1])


def _sb_prompt(qx, kb, vb, u2, tq):
    b, s, _ = qx.shape
    return pl.pallas_call(
        functools.partial(_sb_prompt_kernel, tq=tq),
        grid=(b, s // tq),
        in_specs=[pl.BlockSpec((1, tq, HEADS * LANES), lambda bi, qi: (bi, qi, 0)),
                  pl.BlockSpec((1, s, SB_W), lambda bi, qi: (bi, 0, 0)),
                  pl.BlockSpec((1, s, SB_W), lambda bi, qi: (bi, 0, 0)),
                  _const_spec(u2.shape)],
        out_specs=pl.BlockSpec((1, tq, SB_W), lambda bi, qi: (bi, qi, 0)),
        out_shape=jax.ShapeDtypeStruct((b, s, SB_W), F32),
        scratch_shapes=[pltpu.VMEM((HEADS, tq, LANES), F32), pltpu.VMEM((HEADS * tq, LANES), F32)],
        compiler_params=pltpu.CompilerParams(dimension_semantics=("arbitrary", "arbitrary"),
                                             vmem_limit_bytes=VMEM_LIMIT),
        name="sb_prompt",
    )(qx, kb, vb, u2)


def _softmax_step(s, kv_lat, m_ref, l_ref, acc_ref):
    m_prev = m_ref[:, :1]
    m_new = jnp.maximum(m_prev, jnp.max(s, axis=-1, keepdims=True))
    p = jnp.exp(s - m_new)
    alpha = jnp.exp(m_prev - m_new)
    l_ref[...] = jnp.broadcast_to(alpha * l_ref[:, :1] + jnp.sum(p, axis=-1, keepdims=True), l_ref.shape)
    acc_ref[...] = alpha * acc_ref[...] + _dot(p.astype(BF16), kv_lat)
    m_ref[...] = jnp.broadcast_to(m_new, m_ref.shape)


def _softmax_step_grouped(s, kv_lat, m_ref, l_ref, acc_ref):
    n_groups = s.shape[1] // LANES
    cols = [s[:, g * LANES:(g + 1) * LANES] for g in range(n_groups)]
    m_prev = m_ref[...]
    m_new = jnp.maximum(m_prev, jnp.max(functools.reduce(jnp.maximum, cols), axis=-1, keepdims=True))
    alpha = jnp.exp(m_prev - m_new)
    ps = [jnp.exp(c - m_new) for c in cols]
    l_ref[...] = alpha * l_ref[...] + jnp.sum(functools.reduce(jnp.add, ps), axis=-1, keepdims=True)
    pv = functools.reduce(jnp.add, [_dot(ps[g].astype(BF16), kv_lat[g * LANES:(g + 1) * LANES])
                                    for g in range(n_groups)])
    acc_ref[...] = alpha * acc_ref[...] + pv
    m_ref[...] = m_new


def _mla_prompt_kernel(qm_ref, kv_ref, o_ref, qs_ref, m_ref, l_ref, acc_ref, *, tq, tk):
    qi = pl.program_id(1)
    rows = HEADS * tq
    for hh in range(HEADS):
        qs_ref[hh * tq:(hh + 1) * tq, :] = qm_ref[0, :, hh * 2 * LANES:(hh + 1) * 2 * LANES]
    m_ref[...] = jnp.full_like(m_ref, -jnp.inf)
    l_ref[...] = jnp.zeros_like(l_ref)
    acc_ref[...] = jnp.zeros_like(acc_ref)
    col = lax.broadcasted_iota(jnp.int32, (rows, tk), 1)
    row = lax.broadcasted_iota(jnp.int32, (rows, tk), 0) & (tq - 1)

    def body(j, carry):
        start = pl.multiple_of(j * tk, tk)
        kv = kv_ref[0, pl.ds(start, tk), :]
        s = _dot_nt(qs_ref[...], kv)
        s = jnp.where(col <= row + (qi * tq - j * tk), s, -jnp.inf)
        _softmax_step_grouped(s, kv[:, :KV_LORA], m_ref, l_ref, acc_ref)
        return carry

    lax.fori_loop(0, lax.div((qi + 1) * tq + tk - 1, tk), body, 0)
    out = acc_ref[...] / l_ref[...]
    for hh in range(HEADS):
        o_ref[0, :, hh * LANES:(hh + 1) * LANES] = out[hh * tq:(hh + 1) * tq]


def _mla_prompt(qm, kv, tq, tk):
    b, s, _ = qm.shape
    rows = HEADS * tq
    assert s % tk == 0 and tk % tq == 0
    return pl.pallas_call(
        functools.partial(_mla_prompt_kernel, tq=tq, tk=tk),
        grid=(b, s // tq),
        in_specs=[pl.BlockSpec((1, tq, HEADS * 2 * LANES), lambda bi, qi: (bi, qi, 0)),
                  pl.BlockSpec((1, s, 2 * LANES), lambda bi, qi: (bi, 0, 0))],
        out_specs=pl.BlockSpec((1, tq, HEADS * LANES), lambda bi, qi: (bi, qi, 0)),
        out_shape=jax.ShapeDtypeStruct((b, s, HEADS * LANES), F32),
        scratch_shapes=[pltpu.VMEM((rows, 2 * LANES), BF16), pltpu.VMEM((rows, LANES), F32),
                        pltpu.VMEM((rows, LANES), F32), pltpu.VMEM((rows, LANES), F32)],
        compiler_params=pltpu.CompilerParams(dimension_semantics=("arbitrary", "arbitrary"),
                                             vmem_limit_bytes=VMEM_LIMIT),
        name="mla_prompt",
    )(qm, kv)


def _sb_sample_kernel(pt_ref, q_ref, kn_ref, vn_ref, u2_ref, ck_ref, cv_ref, o_ref,
                      kbuf, vbuf, xk, xv, sem, xsem, acc_ref, car_ref, *, n_pages, t_new, seqs):
    step = pl.program_id(0)
    rows = HEADS * t_new
    u2 = u2_ref[...]
    lane = lax.broadcasted_iota(jnp.int32, (rows, LANES), 1)
    tq = lax.broadcasted_iota(jnp.int32, (rows, LANES), 0) & (t_new - 1)
    out_head = lax.broadcasted_iota(jnp.int32, (t_new, SB_W), 1) // SB_D

    def first_copies(sq):
        cps = []
        for b in range(SB_FIRST_PAGES):
            page = pt_ref[step * seqs + sq, n_pages - 1 - b]
            cps.append(pltpu.make_async_copy(ck_ref.at[page], kbuf.at[sq, b], sem.at[sq, 0]))
            cps.append(pltpu.make_async_copy(cv_ref.at[page], vbuf.at[sq, b], sem.at[sq, 1]))
        return cps

    for sq in range(seqs):
        for cp in first_copies(sq):
            cp.start()

    for sq in range(seqs):
        n = step * seqs + sq
        q = q_ref[sq]
        for cp in first_copies(sq):
            cp.wait()
        zs = [_dot(q, kn_ref[sq])] + [_dot(q, kbuf[sq, b].astype(BF16)) for b in range(SB_FIRST_PAGES)]
        ws, car = _sb_blocks(zs, [lane < tq] + [None] * SB_FIRST_PAGES, u2, jnp.zeros((rows, LANES), F32))
        acc = _dot_nt(ws[0].astype(BF16), vn_ref[sq])
        for b in range(SB_FIRST_PAGES):
            acc += _dot_nt(ws[1 + b].astype(BF16), vbuf[sq, b].astype(BF16))
        acc_ref[...] = acc
        car_ref[...] = car

        def more(state):
            i, _ = state
            page = pt_ref[n, n_pages - 1 - i]
            ck = pltpu.make_async_copy(ck_ref.at[page], xk, xsem.at[0])
            cv = pltpu.make_async_copy(cv_ref.at[page], xv, xsem.at[1])
            ck.start()
            cv.start()
            ck.wait()
            cv.wait()
            (w,), c = _sb_blocks([_dot(q, xk[...].astype(BF16))], [None], u2, car_ref[...])
            acc_ref[...] += _dot_nt(w.astype(BF16), xv[...].astype(BF16))
            car_ref[...] = c
            return i + 1, jnp.max(c) < -SB_DEAD_LOG

        lax.while_loop(lambda s: (s[0] < n_pages) & jnp.logical_not(s[1]), more,
                       (SB_FIRST_PAGES, jnp.max(car) < -SB_DEAD_LOG))
        a = acc_ref[...]
        out = jnp.zeros((t_new, SB_W), F32)
        for hh in range(HEADS):
            out = jnp.where(out_head == hh, a[hh * t_new:(hh + 1) * t_new], out)
        o_ref[sq] = out


def _sb_sample(page_table, qbd, kn, vn, u2, cache_k, cache_v, seqs):
    n, rows, _ = qbd.shape
    t_new = rows // HEADS
    n_pages = page_table.shape[1]
    assert n % seqs == 0 and n_pages >= SB_FIRST_PAGES and cache_k.shape[1:] == (SB_W, KEY_BLOCK)
    per_seq = lambda shape: pl.BlockSpec((seqs,) + shape, lambda i, pt: (i, 0, 0))
    grid_spec = pltpu.PrefetchScalarGridSpec(
        num_scalar_prefetch=1,
        grid=(n // seqs,),
        in_specs=[per_seq((rows, SB_W)), per_seq((SB_W, KEY_BLOCK)), per_seq((SB_W, KEY_BLOCK)),
                  pl.BlockSpec(u2.shape, lambda i, pt: (0, 0)),
                  pl.BlockSpec(memory_space=pl.ANY),
                  pl.BlockSpec(memory_space=pl.ANY)],
        out_specs=per_seq((t_new, SB_W)),
        scratch_shapes=[pltpu.VMEM((seqs, SB_FIRST_PAGES, SB_W, KEY_BLOCK), F32),
                        pltpu.VMEM((seqs, SB_FIRST_PAGES, SB_W, KEY_BLOCK), F32),
                        pltpu.VMEM((SB_W, KEY_BLOCK), F32),
                        pltpu.VMEM((SB_W, KEY_BLOCK), F32),
                        pltpu.SemaphoreType.DMA((seqs, 2)),
                        pltpu.SemaphoreType.DMA((2,)),
                        pltpu.VMEM((rows, SB_W), F32),
                        pltpu.VMEM((rows, LANES), F32)],
    )
    return pl.pallas_call(
        functools.partial(_sb_sample_kernel, n_pages=n_pages, t_new=t_new, seqs=seqs),
        grid_spec=grid_spec,
        out_shape=jax.ShapeDtypeStruct((n, t_new, SB_W), F32),
        compiler_params=pltpu.CompilerParams(dimension_semantics=("arbitrary",), vmem_limit_bytes=VMEM_LIMIT),
        name="sb_sample",
    )(page_table, qbd, kn, vn, u2, cache_k, cache_v)


def _mla_sample_kernel(pt_ref, q_ref, kvn_ref, *refs, group, t_new):
    ckv_refs = refs[:group]
    kpe_refs = refs[group:2 * group]
    o_ref, ckv_all, kpe_all, m_ref, l_ref, acc_ref = refs[2 * group:]
    j = pl.program_id(1)
    rows = HEADS * t_new
    q = q_ref[0]

    @pl.when(j == 0)
    def _():
        m_ref[...] = jnp.full_like(m_ref, -jnp.inf)
        l_ref[...] = jnp.zeros_like(l_ref)
        acc_ref[...] = jnp.zeros_like(acc_ref)
        col = lax.broadcasted_iota(jnp.int32, (rows, LANES), 1)
        tq = lax.broadcasted_iota(jnp.int32, (rows, LANES), 0) & (t_new - 1)
        kvn = kvn_ref[0]
        s = jnp.where(col <= tq, _dot_nt(q, kvn), -jnp.inf)
        _softmax_step(s, kvn[:, :KV_LORA], m_ref, l_ref, acc_ref)

    q_lat = q[:, :KV_LORA]
    q_pe = q[:, KV_LORA:KV_LORA + ROPE]
    for g in range(group):
        ckv_all[g * KEY_BLOCK:(g + 1) * KEY_BLOCK, :] = ckv_refs[g][...].astype(BF16)
        kpe_all[:, g * KEY_BLOCK:(g + 1) * KEY_BLOCK] = kpe_refs[g][...].astype(BF16)
    kv = ckv_all[...]
    s = _dot_nt(q_lat, kv) + _dot(q_pe, kpe_all[...])
    _softmax_step(s, kv, m_ref, l_ref, acc_ref)

    @pl.when(j == pl.num_programs(1) - 1)
    def _():
        o_ref[0] = acc_ref[...] / l_ref[...]


def _mla_sample(page_table, q, kvn, cache_ckv, cache_kpe, group):
    n, rows, _ = q.shape
    t_new = rows // HEADS
    n_pages = page_table.shape[1]
    assert n_pages % group == 0 and cache_ckv.shape[1:] == (KEY_BLOCK, KV_LORA)
    assert cache_kpe.shape[1:] == (ROPE, KEY_BLOCK)

    def page_spec(shape, g):
        return pl.BlockSpec((None,) + shape, lambda i, j, pt: (pt[i, j * group + g], 0, 0))

    grid_spec = pltpu.PrefetchScalarGridSpec(
        num_scalar_prefetch=1,
        grid=(n, n_pages // group),
        in_specs=[pl.BlockSpec((1, rows, 2 * LANES), lambda i, j, pt: (i, 0, 0)),
                  pl.BlockSpec((1, KEY_BLOCK, 2 * LANES), lambda i, j, pt: (i, 0, 0))]
                 + [page_spec((KEY_BLOCK, KV_LORA), g) for g in range(group)]
                 + [page_spec((ROPE, KEY_BLOCK), g) for g in range(group)],
        out_specs=pl.BlockSpec((1, rows, KV_LORA), lambda i, j, pt: (i, 0, 0)),
        scratch_shapes=[pltpu.VMEM((group * KEY_BLOCK, KV_LORA), BF16), pltpu.VMEM((ROPE, group * KEY_BLOCK), BF16),
                        pltpu.VMEM((rows, LANES), F32), pltpu.VMEM((rows, LANES), F32),
                        pltpu.VMEM((rows, KV_LORA), F32)],
    )
    return pl.pallas_call(
        functools.partial(_mla_sample_kernel, group=group, t_new=t_new),
        grid_spec=grid_spec,
        out_shape=jax.ShapeDtypeStruct((n, rows, KV_LORA), F32),
        compiler_params=pltpu.CompilerParams(dimension_semantics=("arbitrary", "arbitrary"),
                                             vmem_limit_bytes=VMEM_LIMIT),
        name="mla_sample",
    )(page_table, q, kvn, *([cache_ckv] * group), *([cache_kpe] * group))


def _post_kernel(sb_ref, lat_ref, x_ref, p_ref, wuv_ref, gsb_ref, gmla_ref, wo_ref, gffn_ref, wg_ref, wu_ref,
                 wd_ref, gple_ref, wpg_ref, wpp_ref, gfin_ref, y_ref, mrg_ref, ff_ref, *, n_chunks):
    mla_o = _dot(lat_ref[...].astype(BF16), wuv_ref[...])
    mrg_ref[:, :SB_W] = _rms(sb_ref[...], gsb_ref[...]).astype(BF16)
    mrg_ref[:, SB_W:] = _rms(mla_o, gmla_ref[...]).astype(BF16)
    x1 = x_ref[...] + _dot(mrg_ref[...], wo_ref[...])
    h = _rms(x1, gffn_ref[...]).astype(BF16)
    ff_ref[...] = jnp.zeros_like(ff_ref)

    def chunk(c, carry):
        g = _dot(h, wg_ref[c])
        u = _dot(h, wu_ref[c])
        a = (g * jax.nn.sigmoid(g) * u).astype(BF16)
        ff_ref[...] += _dot(a, wd_ref[c])
        return carry

    lax.fori_loop(0, n_chunks, chunk, 0)
    x2 = x1 + ff_ref[...]
    gate = jax.nn.sigmoid(_dot(_rms(x2, gple_ref[...]).astype(BF16), wpg_ref[...]))
    x3 = x2 + gate * _dot(p_ref[...].astype(BF16), wpp_ref[...])
    y_ref[...] = _rms(x3, gfin_ref[...])


def _post(sb_o, lat_o, x, p, wuv, gsb, gmla, wo, gffn, wg, wu, wd, gple, wpg, wpp, gfin, tm):
    t, d = x.shape
    row = lambda w: pl.BlockSpec((tm, w), lambda i: (i, 0))

    def resident(a):
        nd = a.ndim
        return pl.BlockSpec(a.shape, lambda i: (0,) * nd, pipeline_mode=pl.Buffered(1))

    weights = (wuv, gsb, gmla, wo, gffn, wg, wu, wd, gple, wpg, wpp, gfin)
    return pl.pallas_call(
        functools.partial(_post_kernel, n_chunks=wg.shape[0]),
        grid=(t // tm,),
        in_specs=[row(SB_W), row(HEADS * LANES), row(d), row(p.shape[1])] + [resident(a) for a in weights],
        out_specs=row(d),
        out_shape=jax.ShapeDtypeStruct((t, d), F32),
        scratch_shapes=[pltpu.VMEM((tm, SB_W + MLA_W), BF16), pltpu.VMEM((tm, d), F32)],
        compiler_params=pltpu.CompilerParams(dimension_semantics=("arbitrary",), vmem_limit_bytes=VMEM_LIMIT),
        name="post",
    )(sb_o, lat_o, x, p, *weights)


def _rope_tables(pos):
    half = ROPE // 2
    inv = ROPE_THETA ** (-2.0 * jnp.arange(half, dtype=F32) / ROPE)
    ang = pos.astype(F32)[:, None] * inv[None, :]
    cos, sin = jnp.cos(ang), jnp.sin(ang)
    pad = jnp.zeros((pos.shape[0], LANES - ROPE), F32)
    return (jnp.concatenate([cos, cos, pad], axis=1), jnp.concatenate([-sin, sin, pad], axis=1))


def _swap_halves(w):
    half = ROPE // 2
    return jnp.concatenate([w[..., half:], w[..., :half]], axis=-1)


def _place(w):
    return jnp.pad(w, [(0, 0)] * (w.ndim - 1) + [(0, LANES - ROPE)])


def _block_diag(w):
    hh, a, b = w.shape
    eye = jnp.eye(hh, dtype=w.dtype)
    return (eye[:, None, :, None] * w[:, :, None, :]).reshape(hh * a, hh * b)


def _token_tile(t):
    for tm in (512, 256, 128):
        if t % tm == 0:
            return tm
    raise ValueError(f"token count {t} is not a multiple of 128")


def _largest_divisor(n, cap):
    return max(g for g in range(1, cap + 1) if n % g == 0)


def kernel(x_prompt, x_sample, cache_sb_k, cache_sb_v, cache_mla_ckv, cache_mla_kpe, page_table, p_prompt, p_sample, norm_attn, w_in, norm_q_a, w_uq, norm_kv_a, w_uk, w_uv, norm_sb_out, norm_mla_out, w_o, norm_ffn, w_gate, w_up, w_down, norm_ple, w_ple_gate, w_ple_proj, norm_final):
    depth = w_in.shape[0]
    assert depth == 1, "single trunk layer"
    b, s, d = x_prompt.shape
    n, t_new, _ = x_sample.shape
    n_phys = cache_sb_k.shape[1]
    n_pages = page_table.shape[1]
    past = n_pages * cache_sb_k.shape[2]
    assert s % KEY_BLOCK == 0 and t_new == 8 and cache_sb_k.shape[2] == KEY_BLOCK

    row2 = lambda g: g.reshape(1, -1).astype(F32)
    wi = w_in[0]
    o_kpe = 3 * SB_W + Q_LORA + KV_LORA
    w_kpe = wi[:, o_kpe:o_kpe + ROPE]
    w1 = jnp.concatenate([wi[:, :o_kpe], _place(w_kpe), _place(_swap_halves(w_kpe))], axis=1).astype(BF16)
    wq = w_uq[0].reshape(Q_LORA, HEADS, NOPE + ROPE)
    wq_pe = wq[:, :, NOPE:]
    w2 = jnp.concatenate([wq[:, :, :NOPE].reshape(Q_LORA, HEADS * NOPE),
                          _place(wq_pe).reshape(Q_LORA, HEADS * LANES),
                          _place(_swap_halves(wq_pe)).reshape(Q_LORA, HEADS * LANES)], axis=1).astype(BF16)
    wuk = _block_diag(jnp.transpose(w_uk[0], (1, 2, 0))).astype(BF16)
    wuv = _block_diag(jnp.transpose(w_uv[0], (1, 0, 2))).astype(BF16)
    n_chunks = w_gate.shape[2] // FF_CHUNK
    wg = jnp.transpose(w_gate[0].reshape(d, n_chunks, FF_CHUNK), (1, 0, 2)).astype(BF16)
    wu = jnp.transpose(w_up[0].reshape(d, n_chunks, FF_CHUNK), (1, 0, 2)).astype(BF16)
    wd = w_down[0].reshape(n_chunks, FF_CHUNK, d).astype(BF16)
    post_w = (wuv, row2(norm_sb_out[0]), row2(norm_mla_out[0]), w_o[0].astype(BF16), row2(norm_ffn[0]), wg, wu, wd,
              row2(norm_ple[0]), w_ple_gate[0].astype(BF16), w_ple_proj[0].astype(BF16), row2(norm_final))
    pre_w = (row2(norm_attn[0]), row2(norm_q_a[0]), row2(norm_kv_a[0]), w1, w2, wuk)
    ki = lax.broadcasted_iota(jnp.int32, (KEY_BLOCK, 2 * LANES), 0)
    si = lax.broadcasted_iota(jnp.int32, (KEY_BLOCK, 2 * LANES), 1)
    u2 = ((ki > si) | (si >= LANES)).astype(BF16)

    tm_p = _token_tile(s)
    cos_p, sin_p = _rope_tables(jnp.arange(s, dtype=jnp.int32))
    xp = x_prompt.reshape(b * s, d)
    qx, k_p, v_p, kb, vb, ckv_p, kpe_p, kv_p, qm = _pre(xp, cos_p, sin_p, s // tm_p, b, *pre_w, tm=tm_p)
    sb_o = _sb_prompt(qx.reshape(b, s, -1), kb.reshape(b, s, -1), vb.reshape(b, s, -1), u2, tq=KEY_BLOCK)
    lat_o = _mla_prompt(qm.reshape(b, s, -1), kv_p.reshape(b, s, -1), tq=KEY_BLOCK,
                        tk=MLA_PROMPT_KEYS if s % MLA_PROMPT_KEYS == 0 else KEY_BLOCK)
    y_p = _post(sb_o.reshape(b * s, -1), lat_o.reshape(b * s, -1), xp, p_prompt[0].reshape(b * s, -1), *post_w,
                tm=tm_p)

    ts = n * t_new
    tm_s = _token_tile(ts)
    cos_s, sin_s = _rope_tables(past + (jnp.arange(tm_s, dtype=jnp.int32) % t_new))
    xs = x_sample.reshape(ts, d)
    qx_s, k_s, v_s, _, _, ckv_s, kpe_s, kv_s, qm_s = _pre(xs, cos_s, sin_s, 1, 1, *pre_w, tm=tm_s)
    q_s = qx_s.reshape(ts, HEADS // 2, 2, LANES).sum(axis=2)
    q_s = jnp.transpose(q_s.reshape(n, t_new, HEADS, SB_D), (0, 2, 1, 3))
    eye = jnp.eye(HEADS, dtype=q_s.dtype)
    qbd = (q_s[:, :, :, None, :] * eye[None, :, None, :, None]).reshape(n, HEADS * t_new, SB_W)

    def new_keys_t(a):
        a = jnp.transpose(a.reshape(HEADS, SB_D, n, t_new), (2, 0, 1, 3)).reshape(n, SB_W, t_new)
        return jnp.pad(a, ((0, 0), (0, 0), (0, KEY_BLOCK - t_new))).astype(BF16)

    ck = jnp.transpose(cache_sb_k.reshape(n_phys, KEY_BLOCK, HEADS, SB_D), (0, 2, 3, 1)).reshape(n_phys, SB_W, KEY_BLOCK)
    cv = jnp.transpose(cache_sb_v.reshape(n_phys, KEY_BLOCK, HEADS, SB_D), (0, 2, 3, 1)).reshape(n_phys, SB_W, KEY_BLOCK)
    sb_os = _sb_sample(page_table, qbd, new_keys_t(k_s), new_keys_t(v_s), u2, ck, cv,
                       seqs=_largest_divisor(n, SB_SEQS_PER_STEP))
    q_m = jnp.transpose(qm_s.reshape(n, t_new, HEADS, 2 * LANES), (0, 2, 1, 3)).reshape(n, HEADS * t_new, 2 * LANES)
    kvn = jnp.pad(kv_s.reshape(n, t_new, 2 * LANES), ((0, 0), (0, KEY_BLOCK - t_new), (0, 0)))
    c_ckv = cache_mla_ckv.reshape(n_phys, KEY_BLOCK, KV_LORA)
    c_kpe = jnp.transpose(cache_mla_kpe.reshape(n_phys, KEY_BLOCK, ROPE), (0, 2, 1))
    lat_os = _mla_sample(page_table, q_m, kvn, c_ckv, c_kpe, _largest_divisor(n_pages, MLA_PAGES_PER_STEP))
    lat_os = jnp.transpose(lat_os.reshape(n, HEADS, t_new, KV_LORA), (0, 2, 1, 3)).reshape(ts, HEADS * KV_LORA)
    y_s = _post(sb_os.reshape(ts, SB_W), lat_os, xs, p_sample[0].reshape(ts, -1), *post_w, tm=tm_s)

    def heads_last(a, lead):
        return jnp.transpose(a, (0, 3, 1, 2)).reshape((1,) + lead + (HEADS, SB_D))

    def rope_last(a, lead):
        return jnp.transpose(a, (0, 2, 1)).reshape((1,) + lead + (ROPE,))

    return (y_p.reshape(b, s, d), y_s.reshape(n, t_new, d),
            heads_last(k_p, (b, s)), heads_last(v_p, (b, s)),
            ckv_p.reshape(1, b, s, KV_LORA), rope_last(kpe_p, (b, s)),
            heads_last(k_s, (n, t_new)), heads_last(v_s, (n, t_new)),
            ckv_s.reshape(1, n, t_new, KV_LORA), rope_last(kpe_s, (n, t_new)))
```

```python
import functools

import jax
import jax.numpy as jnp
from jax import lax
from jax.experimental import pallas as pl
from jax.experimental.pallas import tpu as pltpu

F32 = jnp.float32
BF16 = jnp.bfloat16

LANES = 128
HEADS = 8
SB_D = 64
SB_W = HEADS * SB_D
NOPE = 64
ROPE = 32
Q_LORA = 256
KV_LORA = 128
V_DIM = 64
MLA_W = HEADS * V_DIM
FF_CHUNK = 256
ROPE_THETA = 10000.0
NORM_EPS = 1e-6
SB_SCALE = SB_D ** -0.5
MLA_SCALE = (NOPE + ROPE) ** -0.5
KEY_BLOCK = 128
SB_DEAD_LOG = 104.0
SB_SEQS_PER_STEP = 4
SB_FIRST_PAGES = 2
MLA_PAGES_PER_STEP = 64
MLA_PROMPT_KEYS = 256
VMEM_LIMIT = 56 * 1024 * 1024


def _rms(x, g):
    return x * lax.rsqrt(jnp.mean(x * x, axis=-1, keepdims=True) + NORM_EPS) * g


def _dot(a, b):
    return jnp.dot(a, b, preferred_element_type=F32)


def _dot_nt(a, b):
    return lax.dot_general(a, b, (((1,), (1,)), ((), ())), preferred_element_type=F32)


def _const_spec(shape):
    nd = len(shape)
    return pl.BlockSpec(shape, lambda *_: (0,) * nd)


def _pre_kernel(x_ref, cos_ref, sin_ref, ga_ref, gq_ref, gkv_ref, w1_ref, w2_ref, wuk_ref,
                qx_ref, k_ref, v_ref, kb_ref, vb_ref, ckv_ref, kpe_ref, kv_ref, qm_ref):
    x = x_ref[...]
    tm = x.shape[0]
    h = _rms(x, ga_ref[...]).astype(BF16)
    proj = _dot(h, w1_ref[...])
    lane = lax.broadcasted_iota(jnp.int32, (tm, LANES), 1)
    low = lane < SB_D
    for hp in range(HEADS // 2):
        blk = proj[:, hp * LANES:(hp + 1) * LANES] * SB_SCALE
        qx_ref[:, (2 * hp) * LANES:(2 * hp + 1) * LANES] = jnp.where(low, blk, 0.0).astype(BF16)
        qx_ref[:, (2 * hp + 1) * LANES:(2 * hp + 2) * LANES] = jnp.where(low, 0.0, blk).astype(BF16)
    k = proj[:, SB_W:2 * SB_W]
    v = proj[:, 2 * SB_W:3 * SB_W]
    k_ref[0] = k.T.reshape(HEADS, SB_D, tm)
    v_ref[0] = v.T.reshape(HEADS, SB_D, tm)
    kb_ref[...] = k.astype(BF16)
    vb_ref[...] = v.astype(BF16)
    o = 3 * SB_W
    cqn = _rms(proj[:, o:o + Q_LORA], gq_ref[...]).astype(BF16)
    o += Q_LORA
    ckv = _rms(proj[:, o:o + KV_LORA], gkv_ref[...])
    o += KV_LORA
    cos = cos_ref[...]
    sin = sin_ref[...]
    kpe = proj[:, o:o + LANES] * cos + proj[:, o + LANES:o + 2 * LANES] * sin
    ckv_ref[...] = ckv
    kpe_ref[0] = kpe.T[:ROPE]
    kv_ref[:, :KV_LORA] = ckv.astype(BF16)
    kv_ref[:, KV_LORA:] = kpe.astype(BF16)
    q2 = _dot(cqn, w2_ref[...])
    lat = _dot(q2[:, :HEADS * NOPE].astype(BF16), wuk_ref[...])
    o1 = HEADS * NOPE
    o2 = o1 + HEADS * LANES
    for hh in range(HEADS):
        pe = q2[:, o1 + hh * LANES:o1 + (hh + 1) * LANES] * cos + q2[:, o2 + hh * LANES:o2 + (hh + 1) * LANES] * sin
        qm_ref[:, hh * 2 * LANES:hh * 2 * LANES + LANES] = (lat[:, hh * LANES:(hh + 1) * LANES] * MLA_SCALE).astype(BF16)
        qm_ref[:, hh * 2 * LANES + LANES:(hh + 1) * 2 * LANES] = (pe * MLA_SCALE).astype(BF16)


def _pre(x, cos_t, sin_t, n_tab_blocks, groups, ga, gq, gkv, w1, w2, wuk, tm):
    t, d = x.shape
    grid = (t // tm,)
    per = t // groups
    nper = per // tm
    row = lambda w: pl.BlockSpec((tm, w), lambda i: (i, 0))
    tab = pl.BlockSpec((tm, LANES), lambda i: (i % n_tab_blocks, 0))
    head_t = pl.BlockSpec((1, HEADS, SB_D, tm), lambda i: (i // nper, 0, 0, i % nper))
    outs = [
        (jax.ShapeDtypeStruct((t, HEADS * LANES), BF16), row(HEADS * LANES)),
        (jax.ShapeDtypeStruct((groups, HEADS, SB_D, per), F32), head_t),
        (jax.ShapeDtypeStruct((groups, HEADS, SB_D, per), F32), head_t),
        (jax.ShapeDtypeStruct((t, SB_W), BF16), row(SB_W)),
        (jax.ShapeDtypeStruct((t, SB_W), BF16), row(SB_W)),
        (jax.ShapeDtypeStruct((t, KV_LORA), F32), row(KV_LORA)),
        (jax.ShapeDtypeStruct((groups, ROPE, per), F32),
         pl.BlockSpec((1, ROPE, tm), lambda i: (i // nper, 0, i % nper))),
        (jax.ShapeDtypeStruct((t, 2 * LANES), BF16), row(2 * LANES)),
        (jax.ShapeDtypeStruct((t, HEADS * 2 * LANES), BF16), row(HEADS * 2 * LANES)),
    ]
    return pl.pallas_call(
        _pre_kernel,
        grid=grid,
        in_specs=[row(d), tab, tab, _const_spec(ga.shape), _const_spec(gq.shape), _const_spec(gkv.shape),
                  _const_spec(w1.shape), _const_spec(w2.shape), _const_spec(wuk.shape)],
        out_specs=[o[1] for o in outs],
        out_shape=[o[0] for o in outs],
        compiler_params=pltpu.CompilerParams(dimension_semantics=("arbitrary",), vmem_limit_bytes=VMEM_LIMIT),
        name="pre",
    )(x, cos_t, sin_t, ga, gq, gkv, w1, w2, wuk)


def _sb_blocks(zs, masks, u2, carry):
    ws = []
    for z, mask in zip(zs, masks):
        l1p = jnp.log1p(jnp.exp(-jnp.abs(z)))
        log_beta = jnp.minimum(z, 0.0) - l1p
        log_keep = jnp.minimum(-z, 0.0) - l1p
        if mask is not None:
            log_keep = jnp.where(mask, log_keep, 0.0)
        hi = log_keep.astype(BF16)
        lo = (log_keep - hi.astype(F32)).astype(BF16)
        cs = _dot(hi, u2) + _dot(lo, u2)
        w = jnp.exp(log_beta + cs[:, :LANES] + carry)
        ws.append(w if mask is None else jnp.where(mask, w, 0.0))
        carry = carry + cs[:, LANES:]
    return ws, carry


def _sb_prompt_kernel(q_ref, k_ref, v_ref, u2_ref, o_ref, acc_ref, car_ref, *, tq):
    qi = pl.program_id(1)
    rows = HEADS * tq
    lane = lax.broadcasted_iota(jnp.int32, (rows, LANES), 1)
    row = lax.broadcasted_iota(jnp.int32, (rows, LANES), 0) & (tq - 1)
    acc_ref[...] = jnp.zeros_like(acc_ref)
    car_ref[...] = jnp.zeros_like(car_ref)
    u2 = u2_ref[...]

    def body(state):
        j, _ = state
        mask = lane < row + (qi - j) * KEY_BLOCK
        start = pl.multiple_of(j * KEY_BLOCK, KEY_BLOCK)
        zs = []
        for hh in range(HEADS):
            hp = hh // 2
            kblk = k_ref[0, pl.ds(start, KEY_BLOCK), hp * LANES:(hp + 1) * LANES]
            zs.append(_dot_nt(q_ref[0, :, hh * LANES:(hh + 1) * LANES], kblk))
        (w,), car = _sb_blocks([jnp.concatenate(zs, axis=0)], [mask], u2, car_ref[...])
        car_ref[...] = car
        for hh in range(HEADS):
            hp = hh // 2
            vblk = v_ref[0, pl.ds(start, KEY_BLOCK), hp * LANES:(hp + 1) * LANES]
            acc_ref[hh] += _dot(w[hh * tq:(hh + 1) * tq].astype(BF16), vblk)
        return j - 1, jnp.max(car) < -SB_DEAD_LOG

    lax.while_loop(lambda s: (s[0] >= 0) & jnp.logical_not(s[1]), body, (qi, False))
    low = lax.broadcasted_iota(jnp.int32, (tq, LANES), 1) < SB_D
    for hp in range(HEADS // 2):
        o_ref[0, :, hp * LANES:(hp + 1) * LANES] = jnp.where(low, acc_ref[2 * hp], acc_ref[2 * hp + 1])


def _sb_prompt(qx, kb, vb, u2, tq):
    b, s, _ = qx.shape
    return pl.pallas_call(
        functools.partial(_sb_prompt_kernel, tq=tq),
        grid=(b, s // tq),
        in_specs=[pl.BlockSpec((1, tq, HEADS * LANES), lambda bi, qi: (bi, qi, 0)),
                  pl.BlockSpec((1, s, SB_W), lambda bi, qi: (bi, 0, 0)),
                  pl.BlockSpec((1, s, SB_W), lambda bi, qi: (bi, 0, 0)),
                  _const_spec(u2.shape)],
        out_specs=pl.BlockSpec((1, tq, SB_W), lambda bi, qi: (bi, qi, 0)),
        out_shape=jax.ShapeDtypeStruct((b, s, SB_W), F32),
        scratch_shapes=[pltpu.VMEM((HEADS, tq, LANES), F32), pltpu.VMEM((HEADS * tq, LANES), F32)],
        compiler_params=pltpu.CompilerParams(dimension_semantics=("arbitrary", "arbitrary"),
                                             vmem_limit_bytes=VMEM_LIMIT),
        name="sb_prompt",
    )(qx, kb, vb, u2)


def _softmax_step(s, kv_lat, m_ref, l_ref, acc_ref):
    m_prev = m_ref[:, :1]
    m_new = jnp.maximum(m_prev, jnp.max(s, axis=-1, keepdims=True))
    p = jnp.exp(s - m_new)
    alpha = jnp.exp(m_prev - m_new)
    l_ref[...] = jnp.broadcast_to(alpha * l_ref[:, :1] + jnp.sum(p, axis=-1, keepdims=True), l_ref.shape)
    acc_ref[...] = alpha * acc_ref[...] + _dot(p.astype(BF16), kv_lat)
    m_ref[...] = jnp.broadcast_to(m_new, m_ref.shape)


def _softmax_step_grouped(s, kv_lat, m_ref, l_ref, acc_ref):
    n_groups = s.shape[1] // LANES
    cols = [s[:, g * LANES:(g + 1) * LANES] for g in range(n_groups)]
    m_prev = m_ref[...]
    m_new = jnp.maximum(m_prev, jnp.max(functools.reduce(jnp.maximum, cols), axis=-1, keepdims=True))
    alpha = jnp.exp(m_prev - m_new)
    ps = [jnp.exp(c - m_new) for c in cols]
    l_ref[...] = alpha * l_ref[...] + jnp.sum(functools.reduce(jnp.add, ps), axis=-1, keepdims=True)
    pv = functools.reduce(jnp.add, [_dot(ps[g].astype(BF16), kv_lat[g * LANES:(g + 1) * LANES])
                                    for g in range(n_groups)])
    acc_ref[...] = alpha * acc_ref[...] + pv
    m_ref[...] = m_new


def _mla_prompt_kernel(qm_ref, kv_ref, o_ref, qs_ref, m_ref, l_ref, acc_ref, *, tq, tk):
    qi = pl.program_id(1)
    rows = HEADS * tq
    for hh in range(HEADS):
        qs_ref[hh * tq:(hh + 1) * tq, :] = qm_ref[0, :, hh * 2 * LANES:(hh + 1) * 2 * LANES]
    m_ref[...] = jnp.full_like(m_ref, -jnp.inf)
    l_ref[...] = jnp.zeros_like(l_ref)
    acc_ref[...] = jnp.zeros_like(acc_ref)
    col = lax.broadcasted_iota(jnp.int32, (rows, tk), 1)
    row = lax.broadcasted_iota(jnp.int32, (rows, tk), 0) & (tq - 1)

    def body(j, carry):
        start = pl.multiple_of(j * tk, tk)
        kv = kv_ref[0, pl.ds(start, tk), :]
        s = _dot_nt(qs_ref[...], kv)
        s = jnp.where(col <= row + (qi * tq - j * tk), s, -jnp.inf)
        _softmax_step_grouped(s, kv[:, :KV_LORA], m_ref, l_ref, acc_ref)
        return carry

    lax.fori_loop(0, lax.div((qi + 1) * tq + tk - 1, tk), body, 0)
    out = acc_ref[...] / l_ref[...]
    for hh in range(HEADS):
        o_ref[0, :, hh * LANES:(hh + 1) * LANES] = out[hh * tq:(hh + 1) * tq]


def _mla_prompt(qm, kv, tq, tk):
    b, s, _ = qm.shape
    rows = HEADS * tq
    assert s % tk == 0 and tk % tq == 0
    return pl.pallas_call(
        functools.partial(_mla_prompt_kernel, tq=tq, tk=tk),
        grid=(b, s // tq),
        in_specs=[pl.BlockSpec((1, tq, HEADS * 2 * LANES), lambda bi, qi: (bi, qi, 0)),
                  pl.BlockSpec((1, s, 2 * LANES), lambda bi, qi: (bi, 0, 0))],
        out_specs=pl.BlockSpec((1, tq, HEADS * LANES), lambda bi, qi: (bi, qi, 0)),
        out_shape=jax.ShapeDtypeStruct((b, s, HEADS * LANES), F32),
        scratch_shapes=[pltpu.VMEM((rows, 2 * LANES), BF16), pltpu.VMEM((rows, LANES), F32),
                        pltpu.VMEM((rows, LANES), F32), pltpu.VMEM((rows, LANES), F32)],
        compiler_params=pltpu.CompilerParams(dimension_semantics=("arbitrary", "arbitrary"),
                                             vmem_limit_bytes=VMEM_LIMIT),
        name="mla_prompt",
    )(qm, kv)


def _sb_sample_kernel(pt_ref, q_ref, kn_ref, vn_ref, u2_ref, ck_ref, cv_ref, o_ref,
                      kbuf, vbuf, xk, xv, sem, xsem, acc_ref, car_ref, *, n_pages, t_new, seqs):
    step = pl.program_id(0)
    rows = HEADS * t_new
    u2 = u2_ref[...]
    lane = lax.broadcasted_iota(jnp.int32, (rows, LANES), 1)
    tq = lax.broadcasted_iota(jnp.int32, (rows, LANES), 0) & (t_new - 1)
    out_head = lax.broadcasted_iota(jnp.int32, (t_new, SB_W), 1) // SB_D

    def first_copies(sq):
        cps = []
        for b in range(SB_FIRST_PAGES):
            page = pt_ref[step * seqs + sq, n_pages - 1 - b]
            cps.append(pltpu.make_async_copy(ck_ref.at[page], kbuf.at[sq, b], sem.at[sq, 0]))
            cps.append(pltpu.make_async_copy(cv_ref.at[page], vbuf.at[sq, b], sem.at[sq, 1]))
        return cps

    for sq in range(seqs):
        for cp in first_copies(sq):
            cp.start()

    for sq in range(seqs):
        n = step * seqs + sq
        q = q_ref[sq]
        for cp in first_copies(sq):
            cp.wait()
        zs = [_dot(q, kn_ref[sq])] + [_dot(q, kbuf[sq, b].astype(BF16)) for b in range(SB_FIRST_PAGES)]
        ws, car = _sb_blocks(zs, [lane < tq] + [None] * SB_FIRST_PAGES, u2, jnp.zeros((rows, LANES), F32))
        acc = _dot_nt(ws[0].astype(BF16), vn_ref[sq])
        for b in range(SB_FIRST_PAGES):
            acc += _dot_nt(ws[1 + b].astype(BF16), vbuf[sq, b].astype(BF16))
        acc_ref[...] = acc
        car_ref[...] = car

        def more(state):
            i, _ = state
            page = pt_ref[n, n_pages - 1 - i]
            ck = pltpu.make_async_copy(ck_ref.at[page], xk, xsem.at[0])
            cv = pltpu.make_async_copy(cv_ref.at[page], xv, xsem.at[1])
            ck.start()
            cv.start()
            ck.wait()
            cv.wait()
            (w,), c = _sb_blocks([_dot(q, xk[...].astype(BF16))], [None], u2, car_ref[...])
            acc_ref[...] += _dot_nt(w.astype(BF16), xv[...].astype(BF16))
            car_ref[...] = c
            return i + 1, jnp.max(c) < -SB_DEAD_LOG

        lax.while_loop(lambda s: (s[0] < n_pages) & jnp.logical_not(s[1]), more,
                       (SB_FIRST_PAGES, jnp.max(car) < -SB_DEAD_LOG))
        a = acc_ref[...]
        out = jnp.zeros((t_new, SB_W), F32)
        for hh in range(HEADS):
            out = jnp.where(out_head == hh, a[hh * t_new:(hh + 1) * t_new], out)
        o_ref[sq] = out


def _sb_sample(page_table, qbd, kn, vn, u2, cache_k, cache_v, seqs):
    n, rows, _ = qbd.shape
    t_new = rows // HEADS
    n_pages = page_table.shape[1]
    assert n % seqs == 0 and n_pages >= SB_FIRST_PAGES and cache_k.shape[1:] == (SB_W, KEY_BLOCK)
    per_seq = lambda shape: pl.BlockSpec((seqs,) + shape, lambda i, pt: (i, 0, 0))
    grid_spec = pltpu.PrefetchScalarGridSpec(
        num_scalar_prefetch=1,
        grid=(n // seqs,),
        in_specs=[per_seq((rows, SB_W)), per_seq((SB_W, KEY_BLOCK)), per_seq((SB_W, KEY_BLOCK)),
                  pl.BlockSpec(u2.shape, lambda i, pt: (0, 0)),
                  pl.BlockSpec(memory_space=pl.ANY),
                  pl.BlockSpec(memory_space=pl.ANY)],
        out_specs=per_seq((t_new, SB_W)),
        scratch_shapes=[pltpu.VMEM((seqs, SB_FIRST_PAGES, SB_W, KEY_BLOCK), F32),
                        pltpu.VMEM((seqs, SB_FIRST_PAGES, SB_W, KEY_BLOCK), F32),
                        pltpu.VMEM((SB_W, KEY_BLOCK), F32),
                        pltpu.VMEM((SB_W, KEY_BLOCK), F32),
                        pltpu.SemaphoreType.DMA((seqs, 2)),
                        pltpu.SemaphoreType.DMA((2,)),
                        pltpu.VMEM((rows, SB_W), F32),
                        pltpu.VMEM((rows, LANES), F32)],
    )
    return pl.pallas_call(
        functools.partial(_sb_sample_kernel, n_pages=n_pages, t_new=t_new, seqs=seqs),
        grid_spec=grid_spec,
        out_shape=jax.ShapeDtypeStruct((n, t_new, SB_W), F32),
        compiler_params=pltpu.CompilerParams(dimension_semantics=("arbitrary",), vmem_limit_bytes=VMEM_LIMIT),
        name="sb_sample",
    )(page_table, qbd, kn, vn, u2, cache_k, cache_v)


def _mla_sample_kernel(pt_ref, q_ref, kvn_ref, *refs, group, t_new):
    ckv_refs = refs[:group]
    kpe_refs = refs[group:2 * group]
    o_ref, ckv_all, kpe_all, m_ref, l_ref, acc_ref = refs[2 * group:]
    j = pl.program_id(1)
    rows = HEADS * t_new
    q = q_ref[0]

    @pl.when(j == 0)
    def _():
        m_ref[...] = jnp.full_like(m_ref, -jnp.inf)
        l_ref[...] = jnp.zeros_like(l_ref)
        acc_ref[...] = jnp.zeros_like(acc_ref)
        col = lax.broadcasted_iota(jnp.int32, (rows, LANES), 1)
        tq = lax.broadcasted_iota(jnp.int32, (rows, LANES), 0) & (t_new - 1)
        kvn = kvn_ref[0]
        s = jnp.where(col <= tq, _dot_nt(q, kvn), -jnp.inf)
        _softmax_step(s, kvn[:, :KV_LORA], m_ref, l_ref, acc_ref)

    q_lat = q[:, :KV_LORA]
    q_pe = q[:, KV_LORA:KV_LORA + ROPE]
    for g in range(group):
        ckv_all[g * KEY_BLOCK:(g + 1) * KEY_BLOCK, :] = ckv_refs[g][...].astype(BF16)
        kpe_all[:, g * KEY_BLOCK:(g + 1) * KEY_BLOCK] = kpe_refs[g][...].astype(BF16)
    kv = ckv_all[...]
    s = _dot_nt(q_lat, kv) + _dot(q_pe, kpe_all[...])
    _softmax_step(s, kv, m_ref, l_ref, acc_ref)

    @pl.when(j == pl.num_programs(1) - 1)
    def _():
        o_ref[0] = acc_ref[...] / l_ref[...]


def _mla_sample(page_table, q, kvn, cache_ckv, cache_kpe, group):
    n, rows, _ = q.shape
    t_new = rows // HEADS
    n_pages = page_table.shape[1]
    assert n_pages % group == 0 and cache_ckv.shape[1:] == (KEY_BLOCK, KV_LORA)
    assert cache_kpe.shape[1:] == (ROPE, KEY_BLOCK)

    def page_spec(shape, g):
        return pl.BlockSpec((None,) + shape, lambda i, j, pt: (pt[i, j * group + g], 0, 0))

    grid_spec = pltpu.PrefetchScalarGridSpec(
        num_scalar_prefetch=1,
        grid=(n, n_pages // group),
        in_specs=[pl.BlockSpec((1, rows, 2 * LANES), lambda i, j, pt: (i, 0, 0)),
                  pl.BlockSpec((1, KEY_BLOCK, 2 * LANES), lambda i, j, pt: (i, 0, 0))]
                 + [page_spec((KEY_BLOCK, KV_LORA), g) for g in range(group)]
                 + [page_spec((ROPE, KEY_BLOCK), g) for g in range(group)],
        out_specs=pl.BlockSpec((1, rows, KV_LORA), lambda i, j, pt: (i, 0, 0)),
        scratch_shapes=[pltpu.VMEM((group * KEY_BLOCK, KV_LORA), BF16), pltpu.VMEM((ROPE, group * KEY_BLOCK), BF16),
                        pltpu.VMEM((rows, LANES), F32), pltpu.VMEM((rows, LANES), F32),
                        pltpu.VMEM((rows, KV_LORA), F32)],
    )
    return pl.pallas_call(
        functools.partial(_mla_sample_kernel, group=group, t_new=t_new),
        grid_spec=grid_spec,
        out_shape=jax.ShapeDtypeStruct((n, rows, KV_LORA), F32),
        compiler_params=pltpu.CompilerParams(dimension_semantics=("arbitrary", "arbitrary"),
                                             vmem_limit_bytes=VMEM_LIMIT),
        name="mla_sample",
    )(page_table, q, kvn, *([cache_ckv] * group), *([cache_kpe] * group))


def _post_kernel(sb_ref, lat_ref, x_ref, p_ref, wuv_ref, gsb_ref, gmla_ref, wo_ref, gffn_ref, wg_ref, wu_ref,
                 wd_ref, gple_ref, wpg_ref, wpp_ref, gfin_ref, y_ref, mrg_ref, ff_ref, *, n_chunks):
    mla_o = _dot(lat_ref[...].astype(BF16), wuv_ref[...])
    mrg_ref[:, :SB_W] = _rms(sb_ref[...], gsb_ref[...]).astype(BF16)
    mrg_ref[:, SB_W:] = _rms(mla_o, gmla_ref[...]).astype(BF16)
    x1 = x_ref[...] + _dot(mrg_ref[...], wo_ref[...])
    h = _rms(x1, gffn_ref[...]).astype(BF16)
    ff_ref[...] = jnp.zeros_like(ff_ref)

    def chunk(c, carry):
        g = _dot(h, wg_ref[c])
        u = _dot(h, wu_ref[c])
        a = (g * jax.nn.sigmoid(g) * u).astype(BF16)
        ff_ref[...] += _dot(a, wd_ref[c])
        return carry

    lax.fori_loop(0, n_chunks, chunk, 0)
    x2 = x1 + ff_ref[...]
    gate = jax.nn.sigmoid(_dot(_rms(x2, gple_ref[...]).astype(BF16), wpg_ref[...]))
    x3 = x2 + gate * _dot(p_ref[...].astype(BF16), wpp_ref[...])
    y_ref[...] = _rms(x3, gfin_ref[...])


def _post(sb_o, lat_o, x, p, wuv, gsb, gmla, wo, gffn, wg, wu, wd, gple, wpg, wpp, gfin, tm):
    t, d = x.shape
    row = lambda w: pl.BlockSpec((tm, w), lambda i: (i, 0))

    def resident(a):
        nd = a.ndim
        return pl.BlockSpec(a.shape, lambda i: (0,) * nd, pipeline_mode=pl.Buffered(1))

    weights = (wuv, gsb, gmla, wo, gffn, wg, wu, wd, gple, wpg, wpp, gfin)
    return pl.pallas_call(
        functools.partial(_post_kernel, n_chunks=wg.shape[0]),
        grid=(t // tm,),
        in_specs=[row(SB_W), row(HEADS * LANES), row(d), row(p.shape[1])] + [resident(a) for a in weights],
        out_specs=row(d),
        out_shape=jax.ShapeDtypeStruct((t, d), F32),
        scratch_shapes=[pltpu.VMEM((tm, SB_W + MLA_W), BF16), pltpu.VMEM((tm, d), F32)],
        compiler_params=pltpu.CompilerParams(dimension_semantics=("arbitrary",), vmem_limit_bytes=VMEM_LIMIT),
        name="post",
    )(sb_o, lat_o, x, p, *weights)


def _rope_tables(pos):
    half = ROPE // 2
    inv = ROPE_THETA ** (-2.0 * jnp.arange(half, dtype=F32) / ROPE)
    ang = pos.astype(F32)[:, None] * inv[None, :]
    cos, sin = jnp.cos(ang), jnp.sin(ang)
    pad = jnp.zeros((pos.shape[0], LANES - ROPE), F32)
    return (jnp.concatenate([cos, cos, pad], axis=1), jnp.concatenate([-sin, sin, pad], axis=1))


def _swap_halves(w):
    half = ROPE // 2
    return jnp.concatenate([w[..., half:], w[..., :half]], axis=-1)


def _place(w):
    return jnp.pad(w, [(0, 0)] * (w.ndim - 1) + [(0, LANES - ROPE)])


def _block_diag(w):
    hh, a, b = w.shape
    eye = jnp.eye(hh, dtype=w.dtype)
    return (eye[:, None, :, None] * w[:, :, None, :]).reshape(hh * a, hh * b)


def _token_tile(t):
    for tm in (512, 256, 128):
        if t % tm == 0:
            return tm
    raise ValueError(f"token count {t} is not a multiple of 128")


def _largest_divisor(n, cap):
    return max(g for g in range(1, cap + 1) if n % g == 0)


def kernel(x_prompt, x_sample, cache_sb_k, cache_sb_v, cache_mla_ckv, cache_mla_kpe, page_table, p_prompt, p_sample, norm_attn, w_in, norm_q_a, w_uq, norm_kv_a, w_uk, w_uv, norm_sb_out, norm_mla_out, w_o, norm_ffn, w_gate, w_up, w_down, norm_ple, w_ple_gate, w_ple_proj, norm_final):
    depth = w_in.shape[0]
    assert depth == 1, "single trunk layer"
    b, s, d = x_prompt.shape
    n, t_new, _ = x_sample.shape
    n_phys = cache_sb_k.shape[1]
    n_pages = page_table.shape[1]
    past = n_pages * cache_sb_k.shape[2]
    assert s % KEY_BLOCK == 0 and t_new == 8 and cache_sb_k.shape[2] == KEY_BLOCK

    row2 = lambda g: g.reshape(1, -1).astype(F32)
    wi = w_in[0]
    o_kpe = 3 * SB_W + Q_LORA + KV_LORA
    w_kpe = wi[:, o_kpe:o_kpe + ROPE]
    w1 = jnp.concatenate([wi[:, :o_kpe], _place(w_kpe), _place(_swap_halves(w_kpe))], axis=1).astype(BF16)
    wq = w_uq[0].reshape(Q_LORA, HEADS, NOPE + ROPE)
    wq_pe = wq[:, :, NOPE:]
    w2 = jnp.concatenate([wq[:, :, :NOPE].reshape(Q_LORA, HEADS * NOPE),
                          _place(wq_pe).reshape(Q_LORA, HEADS * LANES),
                          _place(_swap_halves(wq_pe)).reshape(Q_LORA, HEADS * LANES)], axis=1).astype(BF16)
    wuk = _block_diag(jnp.transpose(w_uk[0], (1, 2, 0))).astype(BF16)
    wuv = _block_diag(jnp.transpose(w_uv[0], (1, 0, 2))).astype(BF16)
    n_chunks = w_gate.shape[2] // FF_CHUNK
    wg = jnp.transpose(w_gate[0].reshape(d, n_chunks, FF_CHUNK), (1, 0, 2)).astype(BF16)
    wu = jnp.transpose(w_up[0].reshape(d, n_chunks, FF_CHUNK), (1, 0, 2)).astype(BF16)
    wd = w_down[0].reshape(n_chunks, FF_CHUNK, d).astype(BF16)
    post_w = (wuv, row2(norm_sb_out[0]), row2(norm_mla_out[0]), w_o[0].astype(BF16), row2(norm_ffn[0]), wg, wu, wd,
              row2(norm_ple[0]), w_ple_gate[0].astype(BF16), w_ple_proj[0].astype(BF16), row2(norm_final))
    pre_w = (row2(norm_attn[0]), row2(norm_q_a[0]), row2(norm_kv_a[0]), w1, w2, wuk)
    ki = lax.broadcasted_iota(jnp.int32, (KEY_BLOCK, 2 * LANES), 0)
    si = lax.broadcasted_iota(jnp.int32, (KEY_BLOCK, 2 * LANES), 1)
    u2 = ((ki > si) | (si >= LANES)).astype(BF16)

    tm_p = _token_tile(s)
    cos_p, sin_p = _rope_tables(jnp.arange(s, dtype=jnp.int32))
    xp = x_prompt.reshape(b * s, d)
    qx, k_p, v_p, kb, vb, ckv_p, kpe_p, kv_p, qm = _pre(xp, cos_p, sin_p, s // tm_p, b, *pre_w, tm=tm_p)
    sb_o = _sb_prompt(qx.reshape(b, s, -1), kb.reshape(b, s, -1), vb.reshape(b, s, -1), u2, tq=KEY_BLOCK)
    lat_o = _mla_prompt(qm.reshape(b, s, -1), kv_p.reshape(b, s, -1), tq=KEY_BLOCK,
                        tk=MLA_PROMPT_KEYS if s % MLA_PROMPT_KEYS == 0 else KEY_BLOCK)
    y_p = _post(sb_o.reshape(b * s, -1), lat_o.reshape(b * s, -1), xp, p_prompt[0].reshape(b * s, -1), *post_w,
                tm=tm_p)

    ts = n * t_new
    tm_s = _token_tile(ts)
    cos_s, sin_s = _rope_tables(past + (jnp.arange(tm_s, dtype=jnp.int32) % t_new))
    xs = x_sample.reshape(ts, d)
    qx_s, k_s, v_s, _, _, ckv_s, kpe_s, kv_s, qm_s = _pre(xs, cos_s, sin_s, 1, 1, *pre_w, tm=tm_s)
    q_s = qx_s.reshape(ts, HEADS // 2, 2, LANES).sum(axis=2)
    q_s = jnp.transpose(q_s.reshape(n, t_new, HEADS, SB_D), (0, 2, 1, 3))
    eye = jnp.eye(HEADS, dtype=q_s.dtype)
    qbd = (q_s[:, :, :, None, :] * eye[None, :, None, :, None]).reshape(n, HEADS * t_new, SB_W)

    def new_keys_t(a):
        a = jnp.transpose(a.reshape(HEADS, SB_D, n, t_new), (2, 0, 1, 3)).reshape(n, SB_W, t_new)
        return jnp.pad(a, ((0, 0), (0, 0), (0, KEY_BLOCK - t_new))).astype(BF16)

    ck = jnp.transpose(cache_sb_k.reshape(n_phys, KEY_BLOCK, HEADS, SB_D), (0, 2, 3, 1)).reshape(n_phys, SB_W, KEY_BLOCK)
    cv = jnp.transpose(cache_sb_v.reshape(n_phys, KEY_BLOCK, HEADS, SB_D), (0, 2, 3, 1)).reshape(n_phys, SB_W, KEY_BLOCK)
    sb_os = _sb_sample(page_table, qbd, new_keys_t(k_s), new_keys_t(v_s), u2, ck, cv,
                       seqs=_largest_divisor(n, SB_SEQS_PER_STEP))
    q_m = jnp.transpose(qm_s.reshape(n, t_new, HEADS, 2 * LANES), (0, 2, 1, 3)).reshape(n, HEADS * t_new, 2 * LANES)
    kvn = jnp.pad(kv_s.reshape(n, t_new, 2 * LANES), ((0, 0), (0, KEY_BLOCK - t_new), (0, 0)))
    c_ckv = cache_mla_ckv.reshape(n_phys, KEY_BLOCK, KV_LORA)
    c_kpe = jnp.transpose(cache_mla_kpe.reshape(n_phys, KEY_BLOCK, ROPE), (0, 2, 1))
    lat_os = _mla_sample(page_table, q_m, kvn, c_ckv, c_kpe, _largest_divisor(n_pages, MLA_PAGES_PER_STEP))
    lat_os = jnp.transpose(lat_os.reshape(n, HEADS, t_new, KV_LORA), (0, 2, 1, 3)).reshape(ts, HEADS * KV_LORA)
    y_s = _post(sb_os.reshape(ts, SB_W), lat_os, xs, p_sample[0].reshape(ts, -1), *post_w, tm=tm_s)

    def heads_last(a, lead):
        return jnp.transpose(a, (0, 3, 1, 2)).reshape((1,) + lead + (HEADS, SB_D))

    def rope_last(a, lead):
        return jnp.transpose(a, (0, 2, 1)).reshape((1,) + lead + (ROPE,))

    return (y_p.reshape(b, s, d), y_s.reshape(n, t_new, d),
            heads_last(k_p, (b, s)), heads_last(v_p, (b, s)),
            ckv_p.reshape(1, b, s, KV_LORA), rope_last(kpe_p, (b, s)),
            heads_last(k_s, (n, t_new)), heads_last(v_s, (n, t_new)),
            ckv_s.reshape(1, n, t_new, KV_LORA), rope_last(kpe_s, (n, t_new)))
```
